```python
import math
import jax, jax.numpy as jnp
from jax import lax
import numpy as np

D_MODEL = 2048
BATCH = 2
SEQ = 4096
DEPTH = 1
DEC_BATCH = 32
DEC_SEQ = 1
PAST_LEN = 16384
PAGE_SIZE = 128

N_HEADS = 8
HEAD_DIM = 128
ATTN_WIDTH = N_HEADS * HEAD_DIM
CONV_CH = D_MODEL - ATTN_WIDTH
CONV_WIDTH = 31
MIX_WIDTH = ATTN_WIDTH + CONV_CH
IN_COLS = 3 * ATTN_WIDTH + 2 * CONV_CH
MOBA_BLOCK = 256
MOBA_TOPK = 3
Q_BLOCK = 64
REL_BUCKETS = 32
REL_MAX_DIST = 128
PEER_HEADS = 8
PEER_NKEYS = 128
PEER_EXPERTS = PEER_NKEYS * PEER_NKEYS
PEER_QDIM = 256
PEER_TOPK = 16
PEER_CHUNK = 128
EPS = 1e-6
NEG = -1e30

kernel_name = 'hymba_moba_conformer_peer_step'


def rmsnorm(x, g):
    xf = x.astype(jnp.float32)
    y = xf * lax.rsqrt(jnp.mean(xf * xf, axis=-1, keepdims=True) + EPS)
    return (y * g.astype(jnp.float32)).astype(x.dtype)


def layernorm(x, g, b):
    xf = x.astype(jnp.float32)
    mu = jnp.mean(xf, axis=-1, keepdims=True)
    var = jnp.mean(jnp.square(xf - mu), axis=-1, keepdims=True)
    y = (xf - mu) * lax.rsqrt(var + EPS) * g.astype(jnp.float32) + b.astype(jnp.float32)
    return y.astype(x.dtype)


def rel_bucket(dist):
    n = jnp.maximum(dist, 0)
    max_exact = REL_BUCKETS // 2
    nf = jnp.maximum(n, 1).astype(jnp.float32)
    large = max_exact + (jnp.log(nf / max_exact) / math.log(REL_MAX_DIST / max_exact)
                         * (REL_BUCKETS - max_exact)).astype(jnp.int32)
    return jnp.where(n < max_exact, n, jnp.minimum(large, REL_BUCKETS - 1))


def mixer_inputs(x, norm_g, w_in):
    xn = rmsnorm(x, norm_g)
    proj = xn @ w_in
    q, k, v, glu = jnp.split(proj, [ATTN_WIDTH, 2 * ATTN_WIDTH, 3 * ATTN_WIDTH], axis=-1)
    B, T = x.shape[:2]
    shp = (B, T, N_HEADS, HEAD_DIM)
    return q.reshape(shp), k.reshape(shp), v.reshape(shp), glu


def moba_attend(q, q_pos, k_means, rel_table, gather_kv):
    B, H, Tq, _ = q.shape
    nb = k_means.shape[1]
    own = q_pos // MOBA_BLOCK
    gate = jnp.einsum('bhqd,bnhd->bhqn', q, k_means)
    fully_past = jnp.arange(nb)[None, :] < own[:, None]
    gate = jnp.where(fully_past, gate, NEG)
    _, top_blk = lax.top_k(gate, MOBA_TOPK)
    slot_ok = jnp.arange(MOBA_TOPK)[None, :] < jnp.minimum(own, MOBA_TOPK)[:, None]
    blk = jnp.concatenate([top_blk.astype(jnp.int32),
                           jnp.broadcast_to(own[None, None, :, None], (B, H, Tq, 1))], axis=-1)
    blk_ok = jnp.concatenate([slot_ok, jnp.ones((Tq, 1), dtype=bool)], axis=-1)
    pos = blk[..., None] * MOBA_BLOCK + jnp.arange(MOBA_BLOCK, dtype=jnp.int32)
    k_sel, v_sel = gather_kv(pos)
    dist = q_pos[:, None, None] - pos
    h_idx = jnp.arange(H)[None, :, None, None, None]
    bias = rel_table.astype(jnp.float32).T[h_idx, rel_bucket(dist)]
    logits = jnp.einsum('bhqd,bhqnjd->bhqnj', q, k_sel.astype(jnp.float32)) * (HEAD_DIM ** -0.5) + bias
    mask = blk_ok[:, :, None] & (dist >= 0)
    logits = jnp.where(mask, logits, NEG).reshape(B, H, Tq, -1)
    p = jax.nn.softmax(logits, axis=-1)
    return jnp.einsum('bhqs,bhqsd->bhqd', p,
                      v_sel.astype(jnp.float32).reshape(B, H, Tq, -1, HEAD_DIM))


def prompt_attention(q, k, v, rel_table):
    B, T = q.shape[:2]
    nb = max(-(-T // MOBA_BLOCK), MOBA_TOPK)
    k_pad = jnp.pad(k.astype(jnp.float32), ((0, 0), (0, nb * MOBA_BLOCK - T), (0, 0), (0, 0)))
    k_means = k_pad.reshape(B, nb, MOBA_BLOCK, N_HEADS, HEAD_DIM).mean(axis=2)
    b_idx = jnp.arange(B)[:, None, None, None, None]
    h_idx = jnp.arange(N_HEADS)[None, :, None, None, None]

    def gather_kv(pos):
        p = jnp.minimum(pos, T - 1)
        return k[b_idx, p, h_idx], v[b_idx, p, h_idx]

    n_qb = T // Q_BLOCK
    q_blocks = q.astype(jnp.float32).reshape(B, n_qb, Q_BLOCK, N_HEADS, HEAD_DIM).transpose(1, 0, 3, 2, 4)
    pos_blocks = jnp.arange(T, dtype=jnp.int32).reshape(n_qb, Q_BLOCK)
    out = lax.map(lambda qp: moba_attend(qp[0], qp[1], k_means, rel_table, gather_kv),
                  (q_blocks, pos_blocks))
    return out.transpose(1, 0, 3, 2, 4).reshape(B, T, ATTN_WIDTH)


def sample_attention(q, k_new, v_new, k_pool, v_pool, page_table, rel_table):
    B, S = q.shape[:2]
    n_pages = page_table.shape[1]
    past = n_pages * PAGE_SIZE
    nb = max(-(-(past + S) // MOBA_BLOCK), MOBA_TOPK)
    page_sum = jnp.sum(k_pool, axis=1, dtype=jnp.float32)
    seq_page_sum = page_sum[page_table].transpose(1, 0, 2, 3)
    page_blk = (jnp.arange(n_pages) * PAGE_SIZE) // MOBA_BLOCK
    new_blk = (past + jnp.arange(S)) // MOBA_BLOCK
    blk_sum = (jax.ops.segment_sum(seq_page_sum, page_blk, num_segments=nb)
               + jax.ops.segment_sum(k_new.astype(jnp.float32).transpose(1, 0, 2, 3), new_blk, num_segments=nb))
    k_means = blk_sum.transpose(1, 0, 2, 3) / MOBA_BLOCK
    b_idx = jnp.arange(B)[:, None, None, None, None]
    h_idx = jnp.arange(N_HEADS)[None, :, None, None, None]

    def gather_kv(pos):
        is_past = (pos < past)[..., None]
        phys = page_table[b_idx, jnp.minimum(pos // PAGE_SIZE, n_pages - 1)]
        off = pos % PAGE_SIZE
        ni = jnp.clip(pos - past, 0, S - 1)
        k_sel = jnp.where(is_past, k_pool[phys, off, h_idx], k_new[b_idx, ni, h_idx])
        v_sel = jnp.where(is_past, v_pool[phys, off, h_idx], v_new[b_idx, ni, h_idx])
        return k_sel, v_sel

    q_pos = past + jnp.arange(S, dtype=jnp.int32)
    out = moba_attend(q.astype(jnp.float32).transpose(0, 2, 1, 3), q_pos, k_means, rel_table, gather_kv)
    return out.transpose(0, 2, 1, 3).reshape(B, S, ATTN_WIDTH)


def conformer_conv(glu_in, buf, w_dw, b_dw, ln_g, ln_b):
    a, g = jnp.split(glu_in, 2, axis=-1)
    u = a * jax.nn.sigmoid(g)
    ext = jnp.concatenate([buf.astype(u.dtype), u], axis=1)
    y = lax.conv_general_dilated(ext, w_dw[:, None, :].astype(u.dtype), window_strides=(1,), padding='VALID',
                                 dimension_numbers=('NWC', 'WIO', 'NWC'),
                                 feature_group_count=CONV_CH) + b_dw
    y = layernorm(y, ln_g, ln_b)
    y = y * jax.nn.sigmoid(y)
    return y, ext[:, -(CONV_WIDTH - 1):]


def peer_ffn(xn, wq, sub_keys, u_tab, v_tab):
    n = xn.shape[0]
    q = (xn @ wq).astype(jnp.float32).reshape(n, PEER_HEADS, 2, PEER_QDIM // 2)
    s = jnp.einsum('nhcd,hckd->nhck', q, sub_keys.astype(jnp.float32))
    top_s, top_i = lax.top_k(s, PEER_TOPK)
    cand = top_s[:, :, 0, :, None] + top_s[:, :, 1, None, :]
    best_s, best_c = lax.top_k(cand.reshape(n, PEER_HEADS, -1), PEER_TOPK)
    i1 = jnp.take_along_axis(top_i[:, :, 0], best_c // PEER_TOPK, axis=-1)
    i2 = jnp.take_along_axis(top_i[:, :, 1], best_c % PEER_TOPK, axis=-1)
    expert = i1 * PEER_NKEYS + i2
    gate = jax.nn.softmax(best_s, axis=-1)
    h = jnp.einsum('nd,nhkd->nhk', xn, u_tab[expert]).astype(jnp.float32)
    act = (gate * jax.nn.gelu(h, approximate=False)).astype(xn.dtype)
    return jnp.einsum('nhk,nhkd->nd', act, v_tab[expert])


def setup_inputs(seed: int = 0) -> dict:
    key = jax.random.key(seed)
    ks = jax.random.split(key, 20)
    n_pages = PAST_LEN // PAGE_SIZE
    n_pool = (DEC_BATCH * n_pages * 5) // 4

    def nrm(k, shape, scale):
        return jax.random.normal(k, shape, jnp.float32) * scale

    page_table = jax.random.permutation(ks[0], n_pool)[: DEC_BATCH * n_pages].reshape(DEC_BATCH, n_pages).astype(jnp.int32)
    return {
        'x_prompt': nrm(ks[1], (BATCH, SEQ, D_MODEL), 1.0),
        'x_sample': nrm(ks[2], (DEC_BATCH, DEC_SEQ, D_MODEL), 1.0),
        'cache_k': nrm(ks[3], (DEPTH, n_pool, PAGE_SIZE, N_HEADS, HEAD_DIM), 1.0),
        'cache_v': nrm(ks[4], (DEPTH, n_pool, PAGE_SIZE, N_HEADS, HEAD_DIM), 1.0),
        'state_conv': nrm(ks[5], (DEPTH, DEC_BATCH, CONV_WIDTH - 1, CONV_CH), 0.5),
        'page_table': page_table,
        'rel_bias_table': nrm(ks[6], (REL_BUCKETS, N_HEADS), 0.5),
        'norm_mix_g': 1.0 + nrm(ks[7], (DEPTH, D_MODEL), 0.02),
        'w_in': nrm(ks[8], (DEPTH, D_MODEL, IN_COLS), D_MODEL ** -0.5),
        'w_dw': nrm(ks[9], (DEPTH, CONV_WIDTH, CONV_CH), CONV_WIDTH ** -0.5),
        'b_dw': nrm(ks[10], (DEPTH, CONV_CH), 0.02),
        'conv_ln_g': 1.0 + nrm(ks[11], (DEPTH, CONV_CH), 0.02),
        'conv_ln_b': nrm(ks[12], (DEPTH, CONV_CH), 0.02),
        'w_out': nrm(ks[13], (DEPTH, MIX_WIDTH, D_MODEL), MIX_WIDTH ** -0.5),
        'norm_ffn_g': 1.0 + nrm(ks[14], (DEPTH, D_MODEL), 0.02),
        'peer_wq': nrm(ks[15], (DEPTH, D_MODEL, PEER_HEADS * PEER_QDIM), D_MODEL ** -0.5),
        'peer_sub_keys': nrm(ks[16], (DEPTH, PEER_HEADS, 2, PEER_NKEYS, PEER_QDIM // 2), (PEER_QDIM // 2) ** -0.5),
        'peer_u': nrm(ks[17], (DEPTH, PEER_EXPERTS, D_MODEL), D_MODEL ** -0.5),
        'peer_v': nrm(ks[18], (DEPTH, PEER_EXPERTS, D_MODEL), PEER_HEADS ** -0.5),
        'final_norm_g': 1.0 + nrm(ks[19], (D_MODEL,), 0.02),
    }


def reference(x_prompt, x_sample, cache_k, cache_v, state_conv, page_table,
              rel_bias_table, norm_mix_g, w_in, w_dw, b_dw, conv_ln_g, conv_ln_b, w_out,
              norm_ffn_g, peer_wq, peer_sub_keys, peer_u, peer_v, final_norm_g):
    hp, hs = x_prompt, x_sample
    B, T = hp.shape[:2]
    Bs, S = hs.shape[:2]
    kp_rows, vp_rows, cp_rows, ks_rows, vs_rows, cs_rows = [], [], [], [], [], []
    for l in range(DEPTH):
        qp, kp, vp, gp = mixer_inputs(hp, norm_mix_g[l], w_in[l])
        attn_p = prompt_attention(qp, kp, vp, rel_bias_table)
        conv_p, buf_p = conformer_conv(gp, jnp.zeros((B, CONV_WIDTH - 1, CONV_CH), gp.dtype),
                                       w_dw[l], b_dw[l], conv_ln_g[l], conv_ln_b[l])
        hp = hp + jnp.concatenate([attn_p.astype(hp.dtype), conv_p], axis=-1) @ w_out[l]
        qs, ks_, vs_, gs = mixer_inputs(hs, norm_mix_g[l], w_in[l])
        attn_s = sample_attention(qs, ks_, vs_, cache_k[l], cache_v[l], page_table, rel_bias_table)
        conv_s, buf_s = conformer_conv(gs, state_conv[l], w_dw[l], b_dw[l], conv_ln_g[l], conv_ln_b[l])
        hs = hs + jnp.concatenate([attn_s.astype(hs.dtype), conv_s], axis=-1) @ w_out[l]
        xn_p = rmsnorm(hp, norm_ffn_g[l]).reshape(-1, PEER_CHUNK, D_MODEL)
        ffn_p = lax.map(lambda c: peer_ffn(c, peer_wq[l], peer_sub_keys[l], peer_u[l], peer_v[l]), xn_p)
        hp = hp + ffn_p.reshape(B, T, D_MODEL)
        xn_s = rmsnorm(hs, norm_ffn_g[l]).reshape(Bs * S, D_MODEL)
        hs = hs + peer_ffn(xn_s, peer_wq[l], peer_sub_keys[l], peer_u[l], peer_v[l]).reshape(Bs, S, D_MODEL)
        kp_rows.append(kp)
        vp_rows.append(vp)
        cp_rows.append(buf_p)
        ks_rows.append(ks_)
        vs_rows.append(vs_)
        cs_rows.append(buf_s)
    y_prompt = rmsnorm(hp, final_norm_g)
    y_sample = rmsnorm(hs, final_norm_g)
    k_prompt_new = jnp.stack(kp_rows)
    v_prompt_new = jnp.stack(vp_rows)
    conv_prompt_new = jnp.stack(cp_rows)
    k_sample_new = jnp.stack(ks_rows)
    v_sample_new = jnp.stack(vs_rows)
    conv_sample_new = jnp.stack(cs_rows)
    return (y_prompt, y_sample, k_prompt_new, v_prompt_new, conv_prompt_new, k_sample_new, v_sample_new, conv_sample_new)
```

```python
import functools
import math

import numpy as np
import jax
import jax.numpy as jnp
from jax import lax
from jax.experimental import pallas as pl
from jax.experimental.pallas import tpu as pltpu

F32, BF16, I32 = jnp.float32, jnp.bfloat16, jnp.int32

N_HEADS = 8
HEAD_DIM = 128
ATTN_WIDTH = N_HEADS * HEAD_DIM
MOBA_BLOCK = 256
MOBA_TOPK = 3
PAGE_SIZE = 128
REL_BUCKETS = 32
REL_MAX_DIST = 128
CONV_WIDTH = 31
PEER_HEADS = 8
PEER_NKEYS = 128
PEER_TOPK = 16
PEER_SLOTS = PEER_HEADS * PEER_TOPK
EPS = 1e-6
NEG = -1e30
SCALE = HEAD_DIM ** -0.5

LANES = 128
SUBLANES = 8
VMEM_LIMIT_BYTES = 56 * 1024 * 1024

_NT = (((1,), (1,)), ((), ()))


def _cparams(n_axes):
    return pltpu.CompilerParams(dimension_semantics=("arbitrary",) * n_axes,
                                vmem_limit_bytes=VMEM_LIMIT_BYTES)


def _resident(shape):
    nd = len(shape)
    return pl.BlockSpec(shape, lambda *_: (0,) * nd, pipeline_mode=pl.Buffered(1))


def _rmsnorm(x, g):
    return x * lax.rsqrt(jnp.mean(x * x, axis=-1, keepdims=True) + EPS) * g


def _proj_kernel(x_ref, g_ref, w_ref, q_ref, k_ref, v_ref, u_ref):
    xn = _rmsnorm(x_ref[...], g_ref[...]).astype(BF16)
    c = ATTN_WIDTH

    def mm(j):
        return jnp.dot(xn, w_ref[:, j * c:(j + 1) * c], preferred_element_type=F32)

    q_ref[...] = mm(0)
    k_ref[...] = mm(1)
    v_ref[...] = mm(2)
    u_ref[...] = mm(3) * jax.nn.sigmoid(mm(4))


def _proj(x, g, w_bf16):
    m, d = x.shape
    tm = min(m, 256)
    row = lambda i: (i, 0)
    out = jax.ShapeDtypeStruct((m, ATTN_WIDTH), F32)
    return pl.pallas_call(
        _proj_kernel, out_shape=(out,) * 4, grid=(m // tm,),
        in_specs=[pl.BlockSpec((tm, d), row), _resident((1, d)), _resident(w_bf16.shape)],
        out_specs=(pl.BlockSpec((tm, ATTN_WIDTH), row),) * 4,
        compiler_params=_cparams(1), name="proj")(x, g.reshape(1, d), w_bf16)


def _rel_bucket_of_distance(n_dist):
    n = jnp.arange(n_dist, dtype=I32)
    max_exact = REL_BUCKETS // 2
    nf = jnp.maximum(n, 1).astype(F32)
    large = max_exact + (jnp.log(nf / max_exact) / math.log(REL_MAX_DIST / max_exact)
                         * (REL_BUCKETS - max_exact)).astype(I32)
    return jnp.where(n < max_exact, n, jnp.minimum(large, REL_BUCKETS - 1))


def _bias_kernel(tab_ref, bkt_ref, o_ref):
    h = pl.program_id(0)
    bkt = bkt_ref[...]
    acc = jnp.zeros(bkt.shape, F32)
    for b in range(REL_BUCKETS):
        acc = jnp.where(bkt == b, tab_ref[b, h], acc)
    o_ref[...] = acc


def _bias_lookup(rel_table, bkt):
    r, c = bkt.shape
    return pl.pallas_call(
        _bias_kernel, out_shape=jax.ShapeDtypeStruct((N_HEADS, r, c), F32), grid=(N_HEADS,),
        in_specs=[pl.BlockSpec(memory_space=pltpu.SMEM), _resident((r, c))],
        out_specs=pl.BlockSpec((None, r, c), lambda h: (h, 0, 0)),
        compiler_params=_cparams(1), name="rel_bias")(rel_table, bkt)


def _attn_kernel(nb, tab_ref, q_ref, k_ref, v_ref, bias_ref, o_ref,
                 kb_s, vb_s, km_s, col_s, m_s, l_s, acc_s):
    h = pl.program_id(1)
    i = pl.program_id(2)
    blk = MOBA_BLOCK

    @pl.when(i == 0)
    def _():
        kb_s[...] = k_ref[...].astype(BF16)
        vb_s[...] = v_ref[...].astype(BF16)
        for j in range(nb):
            km_s[j:j + 1, :] = jnp.mean(k_ref[j * blk:(j + 1) * blk, :], axis=0, keepdims=True)

    q = q_ref[...]
    qb = q.astype(BF16)

    gate = lax.dot_general(q, km_s[...], _NT, precision=lax.Precision.HIGHEST,
                           preferred_element_type=F32)
    lane = lax.broadcasted_iota(I32, (blk, nb), 1)
    gate = jnp.where(lane < i, gate, NEG)
    rank = jnp.zeros((blk, nb), I32)
    for jp in range(nb):
        gj = gate[:, jp:jp + 1]
        rank = rank + jnp.where(gj == gate, jnp.where(jp < lane, 1, 0),
                                jnp.where(gj > gate, 1, 0))
    selneg = jnp.where((rank < MOBA_TOPK) & (lane < i), 0.0, NEG)
    for j in range(nb):
        col_s[j] = jnp.broadcast_to(selneg[:, j:j + 1], (blk, LANES))

    def scores(kblk):
        return lax.dot_general(qb, kblk, _NT, preferred_element_type=F32) * SCALE

    r0 = pl.multiple_of(i * blk, blk)
    s = scores(kb_s[pl.ds(r0, blk), :]) + bias_ref[0]
    qi = lax.broadcasted_iota(I32, (blk, blk), 0)
    kj = lax.broadcasted_iota(I32, (blk, blk), 1)
    s = jnp.where(kj <= qi, s, NEG)
    m0 = jnp.max(s, axis=1, keepdims=True)
    p0 = jnp.exp(s - m0)
    m_s[...] = m0
    l_s[...] = jnp.sum(p0, axis=1, keepdims=True)
    acc_s[...] = jnp.dot(p0.astype(BF16), vb_s[pl.ds(r0, blk), :], preferred_element_type=F32)

    def past(j, bias):
        r = pl.multiple_of(j * blk, blk)
        col = col_s[j]
        sj = scores(kb_s[pl.ds(r, blk), :]) + bias + jnp.concatenate([col, col], axis=1)
        m_old = m_s[...]
        m_new = jnp.maximum(m_old, jnp.max(sj, axis=1, keepdims=True))
        alpha = jnp.exp(m_old - m_new)
        p = jnp.exp(sj - m_new)
        l_s[...] = alpha * l_s[...] + jnp.sum(p, axis=1, keepdims=True)
        acc_s[...] = alpha * acc_s[...] + jnp.dot(p.astype(BF16), vb_s[pl.ds(r, blk), :],
                                                  preferred_element_type=F32)
        m_s[...] = m_new

    far_bias = tab_ref[REL_BUCKETS - 1, h]

    def far(j, carry):
        past(j, far_bias)
        return carry

    lax.fori_loop(0, i - 1, far, 0)

    @pl.when(i >= 1)
    def _():
        past(i - 1, bias_ref[1])

    o_ref[...] = acc_s[...] / l_s[...]


def _prompt_attention(q, k, v, bias, rel_table, batch, seq):
    nb = seq // MOBA_BLOCK
    blk = MOBA_BLOCK
    qo = lambda b, h, i: (b * nb + i, h)
    kv = lambda b, h, i: (b, h)
    return pl.pallas_call(
        functools.partial(_attn_kernel, nb),
        out_shape=jax.ShapeDtypeStruct((batch * seq, ATTN_WIDTH), F32),
        grid=(batch, N_HEADS, nb),
        in_specs=[pl.BlockSpec(memory_space=pltpu.SMEM),
                  pl.BlockSpec((blk, HEAD_DIM), qo),
                  pl.BlockSpec((seq, HEAD_DIM), kv),
                  pl.BlockSpec((seq, HEAD_DIM), kv),
                  pl.BlockSpec((None, 2, blk, blk), lambda b, h, i: (h, 0, 0, 0))],
        out_specs=pl.BlockSpec((blk, HEAD_DIM), qo),
        scratch_shapes=[pltpu.VMEM((seq, HEAD_DIM), BF16), pltpu.VMEM((seq, HEAD_DIM), BF16),
                        pltpu.VMEM((nb, HEAD_DIM), F32), pltpu.VMEM((nb, blk, LANES), F32),
                        pltpu.VMEM((blk, 1), F32), pltpu.VMEM((blk, 1), F32),
                        pltpu.VMEM((blk, HEAD_DIM), F32)],
        compiler_params=_cparams(3), name="moba_prompt")(rel_table, q, k, v, bias)


PAGES_PER_STEP = 8


def _sgate_kernel(n_blk, pt_ref, q_ref, *rest):
    pages = rest[:PAGES_PER_STEP]
    g_ref, top_ref = rest[PAGES_PER_STEP:]
    s = pl.program_id(1)
    blocks_per_step = PAGES_PER_STEP // 2

    @pl.when(s == 0)
    def _():
        g_ref[...] = jnp.zeros(g_ref.shape, F32)

    q = q_ref[...]
    lane = lax.broadcasted_iota(I32, (N_HEADS, n_blk), 1)
    acc = g_ref[...]
    for p in range(blocks_per_step):
        bsum = jnp.sum(pages[2 * p][...], axis=0) + jnp.sum(pages[2 * p + 1][...], axis=0)
        gn = jnp.sum(q * (bsum * (1.0 / MOBA_BLOCK)), axis=1, keepdims=True)
        acc = jnp.where(lane == s * blocks_per_step + p, gn, acc)
    g_ref[...] = acc

    @pl.when(s == pl.num_programs(1) - 1)
    def _():
        g = acc
        out_lane = lax.broadcasted_iota(I32, (N_HEADS, LANES), 1)
        top = jnp.zeros((N_HEADS, LANES), I32)
        for t in range(MOBA_TOPK):
            m = jnp.max(g, axis=1, keepdims=True)
            idx = jnp.min(jnp.where(g == m, lane, n_blk), axis=1, keepdims=True)
            top = jnp.where(out_lane == t, idx, top)
            g = jnp.where(lane == idx, -jnp.inf, g)
        top_ref[...] = top


def _sample_gate(q3, cache_k, page_table):
    bs = q3.shape[0]
    n_pages = page_table.shape[1]
    n_blk = n_pages * PAGE_SIZE // MOBA_BLOCK
    assert n_pages % PAGES_PER_STEP == 0 and n_blk >= MOBA_TOPK
    page_shape = (None, PAGE_SIZE, N_HEADS, HEAD_DIM)

    def page_spec(r):
        return pl.BlockSpec(page_shape, lambda b, s, pt: (pt[b, s * PAGES_PER_STEP + r], 0, 0, 0))

    per_b = lambda b, s, pt: (b, 0, 0)
    grid_spec = pltpu.PrefetchScalarGridSpec(
        num_scalar_prefetch=1, grid=(bs, n_pages // PAGES_PER_STEP),
        in_specs=[pl.BlockSpec((None, N_HEADS, HEAD_DIM), per_b)]
        + [page_spec(r) for r in range(PAGES_PER_STEP)],
        out_specs=(pl.BlockSpec((None, N_HEADS, n_blk), per_b),
                   pl.BlockSpec((None, N_HEADS, LANES), per_b)))
    _, top = pl.pallas_call(
        functools.partial(_sgate_kernel, n_blk),
        out_shape=(jax.ShapeDtypeStruct((bs, N_HEADS, n_blk), F32),
                   jax.ShapeDtypeStruct((bs, N_HEADS, LANES), I32)),
        grid_spec=grid_spec, compiler_params=_cparams(2), name="moba_sample_gate")(
            page_table, q3, *([cache_k] * PAGES_PER_STEP))
    return top[:, :, :MOBA_TOPK].reshape(bs, N_HEADS * MOBA_TOPK)


def _sattn_kernel(n_blk, pt_ref, top_ref, tab_ref, q_ref, kn_ref, vn_ref, b_last_ref,
                  k0_ref, k1_ref, v0_ref, v1_ref, o_ref, m_s, l_s, acc_s):
    b = pl.program_id(0)
    h = pl.program_id(1)
    t = pl.program_id(2)
    hm = lax.broadcasted_iota(I32, (N_HEADS, HEAD_DIM), 0) == h

    def head_row(x):
        return jnp.sum(jnp.where(hm[:, :1], x, 0.0), axis=0, keepdims=True)

    q = q_ref[...]
    qh = head_row(q)

    @pl.when(t == 0)
    def _():
        s_self = jnp.sum(qh * head_row(kn_ref[...]), axis=1, keepdims=True) * SCALE + tab_ref[0, h]
        m_s[...] = jnp.broadcast_to(s_self, m_s.shape)
        l_s[...] = jnp.ones(l_s.shape, F32)
        acc_s[...] = head_row(vn_ref[...])

    blk = top_ref[b, h * MOBA_TOPK + t]
    q8 = jnp.where(hm, q, 0.0).astype(BF16)
    far_bias = tab_ref[REL_BUCKETS - 1, h]
    b_last = head_row(b_last_ref[...])
    for r, (kr, vr) in enumerate(((k0_ref, v0_ref), (k1_ref, v1_ref))):
        kh = kr[:, h, :].astype(BF16)
        vh = vr[:, h, :].astype(BF16)
        s8 = lax.dot_general(q8, kh, _NT, preferred_element_type=F32)
        s = head_row(s8) * SCALE
        s = s + jnp.where(blk == n_blk - 1, b_last[:, r * PAGE_SIZE:(r + 1) * PAGE_SIZE], far_bias)
        m_old = m_s[...]
        m_new = jnp.maximum(m_old, jnp.max(s, axis=1, keepdims=True))
        alpha = jnp.exp(m_old - m_new)
        p = jnp.exp(s - m_new)
        l_s[...] = alpha * l_s[...] + jnp.sum(p, axis=1, keepdims=True)
        p8 = jnp.broadcast_to(p, (N_HEADS, PAGE_SIZE)).astype(BF16)
        pv = jnp.dot(p8, vh, preferred_element_type=F32)
        acc_s[...] = alpha * acc_s[...] + pv[:1, :]
        m_s[...] = m_new

    @pl.when(t == MOBA_TOPK - 1)
    def _():
        o_ref[...] = acc_s[...] / l_s[...]


def _sample_attention(q3, kn3, vn3, cache_k, cache_v, page_table, top, rel_table, bias_last):
    bs = q3.shape[0]
    n_blk = page_table.shape[1] * PAGE_SIZE // MOBA_BLOCK
    page_shape = (None, PAGE_SIZE, N_HEADS, HEAD_DIM)

    def page_spec(r):
        return pl.BlockSpec(
            page_shape,
            lambda b, h, t, pt, tp: (pt[b, 2 * tp[b, h * MOBA_TOPK + t] + r], 0, 0, 0))

    per_b = pl.BlockSpec((None, N_HEADS, HEAD_DIM), lambda b, h, t, pt, tp: (b, 0, 0))
    grid_spec = pltpu.PrefetchScalarGridSpec(
        num_scalar_prefetch=2, grid=(bs, N_HEADS, MOBA_TOPK),
        in_specs=[pl.BlockSpec(memory_space=pltpu.SMEM), per_b, per_b, per_b,
                  pl.BlockSpec((N_HEADS, MOBA_BLOCK), lambda b, h, t, pt, tp: (0, 0)),
                  page_spec(0), page_spec(1), page_spec(0), page_spec(1)],
        out_specs=pl.BlockSpec((None, None, 1, HEAD_DIM), lambda b, h, t, pt, tp: (b, h, 0, 0)),
        scratch_shapes=[pltpu.VMEM((1, LANES), F32), pltpu.VMEM((1, LANES), F32),
                        pltpu.VMEM((1, HEAD_DIM), F32)])
    out = pl.pallas_call(
        functools.partial(_sattn_kernel, n_blk),
        out_shape=jax.ShapeDtypeStruct((bs, N_HEADS, 1, HEAD_DIM), F32),
        grid_spec=grid_spec, compiler_params=_cparams(3), name="moba_sample")(
            page_table, top, rel_table, q3, kn3, vn3, bias_last, cache_k, cache_k, cache_v, cache_v)
    return out.reshape(bs, ATTN_WIDTH)


CONV_HALO = 32
CONV_ROWS = 32


def _ln_swish(y, g, b):
    mu = jnp.mean(y, axis=-1, keepdims=True)
    var = jnp.mean(jnp.square(y - mu), axis=-1, keepdims=True)
    yn = (y - mu) * lax.rsqrt(var + EPS) * g + b
    return yn * jax.nn.sigmoid(yn)


def _conv_kernel(tt, cur_ref, halo_ref, w_ref, b_ref, lg_ref, lb_ref, y_ref, ext_s, yc_s):
    t = pl.program_id(1)
    ch = cur_ref.shape[1]
    ext_s[CONV_HALO:, :] = cur_ref[...]

    @pl.when(t == 0)
    def _():
        ext_s[:CONV_HALO, :] = jnp.zeros((CONV_HALO, ch), F32)

    @pl.when(t > 0)
    def _():
        ext_s[:CONV_HALO, :] = halo_ref[...]

    lead = CONV_HALO - (CONV_WIDTH - 1)
    rc = CONV_ROWS

    def chunk(r, carry):
        r0 = pl.multiple_of(r * rc, rc)
        for lt in range(ch // LANES):
            ls = slice(lt * LANES, (lt + 1) * LANES)
            win = ext_s[pl.ds(r0, rc + CONV_HALO), ls]
            acc = jnp.zeros((rc, LANES), F32)
            for sh in range(SUBLANES):
                ws = win if sh == 0 else win[sh:sh + rc + CONV_HALO - SUBLANES]
                for a in range(CONV_HALO // SUBLANES + 1):
                    k = SUBLANES * a + sh - lead
                    if 0 <= k < CONV_WIDTH and SUBLANES * a + rc <= ws.shape[0]:
                        acc = acc + w_ref[k:k + 1, ls] * ws[SUBLANES * a:SUBLANES * a + rc]
            yc_s[pl.ds(r0, rc), ls] = acc + b_ref[:, ls]
        return carry

    lax.fori_loop(0, tt // rc, chunk, 0)
    y_ref[...] = _ln_swish(yc_s[...], lg_ref[...], lb_ref[...])


def _conv_prompt(u, w_dw, b_dw, ln_g, ln_b, batch, seq):
    ch = u.shape[1]
    tt = min(seq, 512)
    nt = seq // tt
    hpt = tt // CONV_HALO
    vec = lambda x: x.reshape(1, ch)
    return pl.pallas_call(
        functools.partial(_conv_kernel, tt),
        out_shape=jax.ShapeDtypeStruct((batch * seq, ch), F32), grid=(batch, nt),
        in_specs=[pl.BlockSpec((tt, ch), lambda b, t: (b * nt + t, 0)),
                  pl.BlockSpec((CONV_HALO, ch), lambda b, t: (jnp.maximum((b * nt + t) * hpt - 1, 0), 0)),
                  _resident(w_dw.shape), _resident((1, ch)), _resident((1, ch)), _resident((1, ch))],
        out_specs=pl.BlockSpec((tt, ch), lambda b, t: (b * nt + t, 0)),
        scratch_shapes=[pltpu.VMEM((tt + CONV_HALO, ch), F32), pltpu.VMEM((tt, ch), F32)],
        compiler_params=_cparams(2), name="conv_prompt")(u, u, w_dw, vec(b_dw), vec(ln_g), vec(ln_b))


def _conv_step_kernel(st_ref, u_ref, w_ref, b_ref, lg_ref, lb_ref, y_ref, ns_ref):
    hist = CONV_WIDTH - 1
    u = u_ref[...]
    acc = w_ref[hist:hist + 1, :] * u
    for k in range(hist):
        acc = acc + w_ref[k:k + 1, :] * st_ref[:, k, :]
    y_ref[...] = _ln_swish(acc + b_ref[...], lg_ref[...], lb_ref[...])
    for k in range(hist - 1):
        ns_ref[:, k, :] = st_ref[:, k + 1, :]
    ns_ref[:, hist - 1, :] = u


def _conv_step(state, u, w_dw, b_dw, ln_g, ln_b):
    bs, hist, ch = state.shape
    vec = lambda x: x.reshape(1, ch)
    return pl.pallas_call(
        _conv_step_kernel,
        out_shape=(jax.ShapeDtypeStruct((bs, ch), F32), jax.ShapeDtypeStruct((bs, hist, ch), F32)),
        compiler_params=pltpu.CompilerParams(vmem_limit_bytes=VMEM_LIMIT_BYTES),
        name="conv_step")(state, u, w_dw, vec(b_dw), vec(ln_g), vec(ln_b))


def _mix_kernel(attn_ref, conv_ref, x_ref, wo_ref, g_ref, wq_ref, h_ref, xn_ref, pq_ref):
    mix = jnp.concatenate([attn_ref[...], conv_ref[...]], axis=1).astype(BF16)
    h = x_ref[...] + jnp.dot(mix, wo_ref[...], preferred_element_type=F32)
    h_ref[...] = h
    xn = _rmsnorm(h, g_ref[...]).astype(BF16)
    xn_ref[...] = xn
    pq_ref[...] = jnp.dot(xn, wq_ref[...], preferred_element_type=F32)


def _mix_out(attn, conv, x, wo_bf16, g, wq_bf16):
    m, d = x.shape
    tm = min(m, 256)
    row = lambda i: (i, 0)
    half = attn.shape[1]
    return pl.pallas_call(
        _mix_kernel,
        out_shape=(jax.ShapeDtypeStruct((m, d), F32), jax.ShapeDtypeStruct((m, d), BF16),
                   jax.ShapeDtypeStruct((m, wq_bf16.shape[1]), F32)),
        grid=(m // tm,),
        in_specs=[pl.BlockSpec((tm, half), row), pl.BlockSpec((tm, half), row), pl.BlockSpec((tm, d), row),
                  _resident(wo_bf16.shape), _resident((1, d)), _resident(wq_bf16.shape)],
        out_specs=(pl.BlockSpec((tm, d), row), pl.BlockSpec((tm, d), row),
                   pl.BlockSpec((tm, wq_bf16.shape[1]), row)),
        compiler_params=_cparams(1), name="mix_out")(attn, conv, x, wo_bf16, g.reshape(1, d), wq_bf16)


ROUTE_TOKENS = LANES
_BIG = 1 << 20


def _candidate_layout():
    k = PEER_TOPK
    groups, flat = [], []
    for a in range(k // 2):
        nb = k // (a + 1)
        rows = -(-nb // SUBLANES) * SUBLANES
        groups.append((a, 1, rows))
        flat += [a * k + b if b < nb else _BIG for b in range(rows)]
    groups.append((k // 2, k // 2, 1))
    flat += [a * k for a in range(k // 2, k)]
    return groups, np.asarray(flat, np.int32)


def _route_kernel(pq_ref, sk_ref, flat_ref, e_ref, g_ref, et_s, gt_s):
    k = PEER_TOPK
    nk = PEER_NKEYS
    tn = ROUTE_TOKENS
    row = lax.broadcasted_iota(I32, (nk, tn), 0)
    groups, _ = _candidate_layout()
    flat = flat_ref[...]

    def half_topk(h, c):
        off = pl.multiple_of((h * 2 + c) * nk, nk)
        s = lax.dot_general(sk_ref[h, c], pq_ref[:, pl.ds(off, nk)], _NT,
                            precision=lax.Precision.HIGHEST, preferred_element_type=F32)
        ts, ti = [], []
        for _ in range(k):
            m = jnp.max(s, axis=0, keepdims=True)
            idx = jnp.min(jnp.where(s == m, row, nk), axis=0, keepdims=True)
            ts.append(m)
            ti.append(idx)
            s = jnp.where(row == idx, -jnp.inf, s)
        return jnp.concatenate(ts, axis=0), jnp.concatenate(ti, axis=0)

    def head(h, carry):
        s0, i0 = half_topk(h, 0)
        s1, i1 = half_topk(h, 1)
        cs, ce = [], []
        for a0, na, nb in groups:
            if na == 1:
                cs.append(s0[a0:a0 + 1] + s1[:nb])
                ce.append(i0[a0:a0 + 1] * nk + i1[:nb])
            else:
                cs.append(s0[a0:a0 + na] + s1[:1])
                ce.append(i0[a0:a0 + na] * nk + i1[:1])
        cand = jnp.where(flat < _BIG, jnp.concatenate(cs, axis=0), -jnp.inf)
        cexp = jnp.concatenate(ce, axis=0)
        bs, be = [], []
        for _ in range(k):
            m = jnp.max(cand, axis=0, keepdims=True)
            idx = jnp.min(jnp.where(cand == m, flat, _BIG), axis=0, keepdims=True)
            hit = flat == idx
            bs.append(m)
            be.append(jnp.max(jnp.where(hit, cexp, -1), axis=0, keepdims=True))
            cand = jnp.where(hit, -jnp.inf, cand)
        best = jnp.concatenate(bs, axis=0)
        p = jnp.exp(best - best[:1])
        r = pl.multiple_of(h * k, k)
        gt_s[pl.ds(r, k), :] = p / jnp.sum(p, axis=0, keepdims=True)
        et_s[pl.ds(r, k), :] = jnp.concatenate(be, axis=0)
        return carry

    lax.fori_loop(0, PEER_HEADS, head, 0)
    e_ref[...] = et_s[...].T
    g_ref[...] = gt_s[...].T


def _route(pq, sub_keys):
    n, qd = pq.shape
    tn = ROUTE_TOKENS
    _, flat = _candidate_layout()
    flat = jnp.asarray(np.broadcast_to(flat[:, None], (flat.shape[0], tn)))
    row = lambda i: (i, 0)
    return pl.pallas_call(
        _route_kernel,
        out_shape=(jax.ShapeDtypeStruct((n, PEER_SLOTS), I32), jax.ShapeDtypeStruct((n, PEER_SLOTS), F32)),
        grid=(n // tn,),
        in_specs=[pl.BlockSpec((tn, qd), row), _resident(sub_keys.shape), _resident(flat.shape)],
        out_specs=(pl.BlockSpec((tn, PEER_SLOTS), row), pl.BlockSpec((tn, PEER_SLOTS), row)),
        scratch_shapes=[pltpu.VMEM((PEER_SLOTS, tn), I32), pltpu.VMEM((PEER_SLOTS, tn), F32)],
        compiler_params=_cparams(1), name="peer_route")(pq, sub_keys, flat)


EXPERT_CHUNK = 1024
_NKEYS_BITS = PEER_NKEYS.bit_length() - 1
assert 1 << _NKEYS_BITS == PEER_NKEYS


def _split_expert(e):
    return lax.shift_right_logical(e, _NKEYS_BITS), e & (PEER_NKEYS - 1)


def _peer_up_kernel(xn_ref, e_ref, u_ref, hs_ref):
    c = pl.program_id(1)
    groups = EXPERT_CHUNK // PEER_NKEYS

    @pl.when(c == 0)
    def _():
        hs_ref[...] = jnp.zeros(hs_ref.shape, F32)

    hd = lax.dot_general(xn_ref[...], u_ref[...], _NT, preferred_element_type=F32)
    e = e_ref[...]
    i1, i2 = _split_expert(e)
    acc = hs_ref[...]
    for g in range(groups):
        got = jnp.take_along_axis(hd[:, g * PEER_NKEYS:(g + 1) * PEER_NKEYS], i2, axis=1)
        acc = jnp.where(i1 == c * groups + g, got, acc)
    hs_ref[...] = acc


def _peer_up(xn_bf16, e, u_bf16):
    n, d = xn_bf16.shape
    n_exp = u_bf16.shape[0]
    tn = 1024 if n % 1024 == 0 else (256 if n % 256 == 0 else n)
    return pl.pallas_call(
        _peer_up_kernel, out_shape=jax.ShapeDtypeStruct((n, PEER_SLOTS), F32),
        grid=(n // tn, n_exp // EXPERT_CHUNK),
        in_specs=[pl.BlockSpec((tn, d), lambda i, c: (i, 0)),
                  pl.BlockSpec((tn, PEER_SLOTS), lambda i, c: (i, 0)),
                  pl.BlockSpec((EXPERT_CHUNK, d), lambda i, c: (c, 0))],
        out_specs=pl.BlockSpec((tn, PEER_SLOTS), lambda i, c: (i, 0)),
        compiler_params=_cparams(2), name="peer_up")(xn_bf16, e, u_bf16)


def _peer_down_kernel(tn, e_ref, gate_ref, hs_ref, v_ref, h_ref, gf_ref, y_ref, act_s, a_s, acc_s):
    c = pl.program_id(1)
    nk = PEER_NKEYS
    groups = EXPERT_CHUNK // nk

    @pl.when(c == 0)
    def _():
        hv = hs_ref[...]
        gelu = 0.5 * hv * (1.0 + lax.erf(hv * math.sqrt(0.5)))
        act_s[...] = gate_ref[...] * gelu
        sub = lax.broadcasted_iota(I32, (nk, PEER_SLOTS), 0)

        def token(n, carry):
            e = e_ref[pl.ds(n, 1), :]
            a = act_s[pl.ds(n, 1), :]
            i1, i2 = _split_expert(e)
            pt = jnp.where(sub == i1, a, 0.0).astype(BF16)
            qt = jnp.where(sub == i2, 1.0, 0.0).astype(BF16)
            a_s[pl.ds(pl.multiple_of(n * nk, nk), nk), :] = lax.dot_general(
                pt, qt, _NT, preferred_element_type=F32)
            return carry

        lax.fori_loop(0, tn, token, 0)
        acc_s[...] = jnp.zeros(acc_s.shape, F32)

    lhs = jnp.concatenate(
        [a_s[pl.ds(c * groups + g, tn, stride=nk), :].astype(BF16) for g in range(groups)], axis=1)
    acc_s[...] += jnp.dot(lhs, v_ref[...], preferred_element_type=F32)

    @pl.when(c == pl.num_programs(1) - 1)
    def _():
        y_ref[...] = _rmsnorm(h_ref[...] + acc_s[...], gf_ref[...])


def _peer_down(e, gate, hs, v_bf16, h, g_final):
    n, d = h.shape
    n_exp = v_bf16.shape[0]
    tn = 256 if n % 256 == 0 else n
    tok = lambda i, c: (i, 0)
    return pl.pallas_call(
        functools.partial(_peer_down_kernel, tn),
        out_shape=jax.ShapeDtypeStruct((n, d), F32), grid=(n // tn, n_exp // EXPERT_CHUNK),
        in_specs=[pl.BlockSpec((tn, PEER_SLOTS), tok), pl.BlockSpec((tn, PEER_SLOTS), tok),
                  pl.BlockSpec((tn, PEER_SLOTS), tok),
                  pl.BlockSpec((EXPERT_CHUNK, d), lambda i, c: (c, 0)),
                  pl.BlockSpec((tn, d), tok), _resident((1, d))],
        out_specs=pl.BlockSpec((tn, d), tok),
        scratch_shapes=[pltpu.VMEM((tn, PEER_SLOTS), F32), pltpu.VMEM((tn * PEER_NKEYS, PEER_NKEYS), F32),
                        pltpu.VMEM((tn, d), F32)],
        compiler_params=_cparams(2), name="peer_down")(e, gate, hs, v_bf16, h, g_final.reshape(1, d))


def _peer_and_final_norm(h, xn_bf16, pq, sub_keys, u_bf16, v_bf16, g_final):
    n = h.shape[0]
    n_pad = -(-n // ROUTE_TOKENS) * ROUTE_TOKENS
    pq_pad = pq if n_pad == n else jnp.pad(pq, ((0, n_pad - n), (0, 0)))
    e, gate = _route(pq_pad, sub_keys)
    e, gate = e[:n], gate[:n]
    hs = _peer_up(xn_bf16, e, u_bf16)
    return _peer_down(e, gate, hs, v_bf16, h, g_final)


def kernel(x_prompt, x_sample, cache_k, cache_v, state_conv, page_table, rel_bias_table, norm_mix_g, w_in,
           w_dw, b_dw, conv_ln_g, conv_ln_b, w_out, norm_ffn_g, peer_wq, peer_sub_keys, peer_u, peer_v,
           final_norm_g):
    batch, seq, d = x_prompt.shape
    bs, s_new, _ = x_sample.shape
    depth = w_in.shape[0]
    assert s_new == 1 and depth == 1 and seq % MOBA_BLOCK == 0 and seq // MOBA_BLOCK >= MOBA_TOPK
    hist = CONV_WIDTH - 1
    blk = MOBA_BLOCK

    bucket = _rel_bucket_of_distance(2 * blk)
    qi = np.arange(blk)[:, None]
    kj = np.arange(blk)[None, :]
    bkt_prompt = jnp.concatenate([bucket[np.maximum(qi - kj, 0)], bucket[qi - kj + blk]], axis=0)
    bias_prompt = _bias_lookup(rel_bias_table, bkt_prompt).reshape(N_HEADS, 2, blk, blk)
    bkt_last = jnp.broadcast_to(bucket[blk - np.arange(blk)][None, :], (SUBLANES, blk))
    bias_last = _bias_lookup(rel_bias_table, bkt_last)[:, 0, :]

    hp = x_prompt.reshape(batch * seq, d)
    hs_ = x_sample.reshape(bs, d)
    l = 0
    w_in_b = w_in[l].astype(BF16)
    w_out_b = w_out[l].astype(BF16)
    wq_b = peer_wq[l].astype(BF16)
    u_b = peer_u[l].astype(BF16)
    v_b = peer_v[l].astype(BF16)

    qp, kp, vp, up = _proj(hp, norm_mix_g[l], w_in_b)
    attn_p = _prompt_attention(qp, kp, vp, bias_prompt, rel_bias_table, batch, seq)
    conv_p = _conv_prompt(up, w_dw[l], b_dw[l], conv_ln_g[l], conv_ln_b[l], batch, seq)
    h_p, xn_p, pq_p = _mix_out(attn_p, conv_p, hp, w_out_b, norm_ffn_g[l], wq_b)
    y_p = _peer_and_final_norm(h_p, xn_p, pq_p, peer_sub_keys[l], u_b, v_b, final_norm_g)

    qs, ks, vs, us = _proj(hs_, norm_mix_g[l], w_in_b)
    q3 = qs.reshape(bs, N_HEADS, HEAD_DIM)
    top = _sample_gate(q3, cache_k[l], page_table)
    attn_s = _sample_attention(q3, ks.reshape(q3.shape), vs.reshape(q3.shape), cache_k[l], cache_v[l],
                               page_table, top, rel_bias_table, bias_last)
    conv_s, state_new = _conv_step(state_conv[l], us, w_dw[l], b_dw[l], conv_ln_g[l], conv_ln_b[l])
    h_s, xn_s, pq_s = _mix_out(attn_s, conv_s, hs_, w_out_b, norm_ffn_g[l], wq_b)
    y_s = _peer_and_final_norm(h_s, xn_s, pq_s, peer_sub_keys[l], u_b, v_b, final_norm_g)

    kv_p = (1, batch, seq, N_HEADS, HEAD_DIM)
    kv_s = (1, bs, 1, N_HEADS, HEAD_DIM)
    conv_prompt_new = up.reshape(batch, seq, -1)[:, seq - hist:, :][None]
    return (y_p.reshape(batch, seq, d), y_s.reshape(bs, 1, d),
            kp.reshape(kv_p), vp.reshape(kv_p), conv_prompt_new,
            ks.reshape(kv_s), vs.reshape(kv_s), state_new[None])
```

```python
import functools
import math

import numpy as np
import jax
import jax.numpy as jnp
from jax import lax
from jax.experimental import pallas as pl
from jax.experimental.pallas import tpu as pltpu

F32, BF16, I32 = jnp.float32, jnp.bfloat16, jnp.int32

N_HEADS = 8
HEAD_DIM = 128
ATTN_WIDTH = N_HEADS * HEAD_DIM
MOBA_BLOCK = 256
MOBA_TOPK = 3
PAGE_SIZE = 128
REL_BUCKETS = 32
REL_MAX_DIST = 128
CONV_WIDTH = 31
PEER_HEADS = 8
PEER_NKEYS = 128
PEER_TOPK = 16
PEER_SLOTS = PEER_HEADS * PEER_TOPK
EPS = 1e-6
NEG = -1e30
SCALE = HEAD_DIM ** -0.5

LANES = 128
SUBLANES = 8
VMEM_LIMIT_BYTES = 56 * 1024 * 1024

_NT = (((1,), (1,)), ((), ()))


def _cparams(n_axes):
    return pltpu.CompilerParams(dimension_semantics=("arbitrary",) * n_axes,
                                vmem_limit_bytes=VMEM_LIMIT_BYTES)


def _resident(shape):
    nd = len(shape)
    return pl.BlockSpec(shape, lambda *_: (0,) * nd, pipeline_mode=pl.Buffered(1))


def _rmsnorm(x, g):
    return x * lax.rsqrt(jnp.mean(x * x, axis=-1, keepdims=True) + EPS) * g


def _proj_kernel(x_ref, g_ref, w_ref, q_ref, k_ref, v_ref, u_ref):
    xn = _rmsnorm(x_ref[...], g_ref[...]).astype(BF16)
    c = ATTN_WIDTH

    def mm(j):
        return jnp.dot(xn, w_ref[:, j * c:(j + 1) * c], preferred_element_type=F32)

    q_ref[...] = mm(0)
    k_ref[...] = mm(1)
    v_ref[...] = mm(2)
    u_ref[...] = mm(3) * jax.nn.sigmoid(mm(4))


def _proj(x, g, w_bf16):
    m, d = x.shape
    tm = min(m, 256)
    row = lambda i: (i, 0)
    out = jax.ShapeDtypeStruct((m, ATTN_WIDTH), F32)
    return pl.pallas_call(
        _proj_kernel, out_shape=(out,) * 4, grid=(m // tm,),
        in_specs=[pl.BlockSpec((tm, d), row), _resident((1, d)), _resident(w_bf16.shape)],
        out_specs=(pl.BlockSpec((tm, ATTN_WIDTH), row),) * 4,
        compiler_params=_cparams(1), name="proj")(x, g.reshape(1, d), w_bf16)


MASKED_BUCKET = REL_BUCKETS


def _rel_bucket(dist):
    dist = jnp.asarray(dist, I32)
    n = jnp.maximum(dist, 0)
    max_exact = REL_BUCKETS // 2
    nf = jnp.maximum(n, 1).astype(F32)
    large = max_exact + (jnp.log(nf / max_exact) / math.log(REL_MAX_DIST / max_exact)
                         * (REL_BUCKETS - max_exact)).astype(I32)
    bucket = jnp.where(n < max_exact, n, jnp.minimum(large, REL_BUCKETS - 1))
    return jnp.where(dist < 0, MASKED_BUCKET, bucket)


def _bias_kernel(tab_ref, bkt_ref, o_ref):
    h = pl.program_id(0)
    bkt = bkt_ref[...]
    acc = jnp.full(bkt.shape, NEG, F32)
    for b in range(REL_BUCKETS):
        acc = jnp.where(bkt == b, tab_ref[b, h], acc)
    o_ref[...] = acc


def _bias_lookup(rel_table, bkt):
    r, c = bkt.shape
    return pl.pallas_call(
        _bias_kernel, out_shape=jax.ShapeDtypeStruct((N_HEADS, r, c), F32), grid=(N_HEADS,),
        in_specs=[pl.BlockSpec(memory_space=pltpu.SMEM), _resident((r, c))],
        out_specs=pl.BlockSpec((None, r, c), lambda h: (h, 0, 0)),
        compiler_params=_cparams(1), name="rel_bias")(rel_table, bkt)


ATTN_WIDE_GROUP = 4
ATTN_NARROW_GROUP = 2


def _attn_kernel(nb, tab_ref, q_ref, k_ref, v_ref, bias_ref, o_ref,
                 kb_s, vt_s, km_s, far_s, m_s, l_s, acc_s):
    h = pl.program_id(1)
    i = pl.program_id(2)
    blk = MOBA_BLOCK

    @pl.when(i == 0)
    def _():
        kb_s[...] = k_ref[...].astype(BF16)
        for j in range(nb):
            rows = slice(j * blk, (j + 1) * blk)
            km_s[j:j + 1, :] = jnp.mean(k_ref[rows, :], axis=0, keepdims=True)
            vt_s[:, rows] = v_ref[rows, :].T.astype(BF16)

    q = q_ref[...]
    qt = q.T.astype(BF16)

    gate = lax.dot_general(km_s[...], q, _NT, precision=lax.Precision.HIGHEST,
                           preferred_element_type=F32)
    sub = lax.broadcasted_iota(I32, (nb, blk), 0)
    gate = jnp.where(sub < i, gate, NEG)
    rank = jnp.zeros((nb, blk), I32)
    for jp in range(nb):
        gj = gate[jp:jp + 1, :]
        rank = rank + jnp.where(gj == gate, jnp.where(jp < sub, 1, 0),
                                jnp.where(gj > gate, 1, 0))
    selneg = jnp.where((rank < MOBA_TOPK) & (sub < i), 0.0, NEG)
    far_s[...] = jnp.where(sub < i - 1, selneg + tab_ref[REL_BUCKETS - 1, h], NEG)
    prev_row = jnp.sum(jnp.where(sub == i - 1, selneg, 0.0), axis=0, keepdims=True)
    prev_row = jnp.where(i >= 1, prev_row, NEG)

    def partial_softmax(blocks):
        rows = [pl.multiple_of(j * blk, blk) for j, _ in blocks]
        sts = [jnp.dot(kb_s[pl.ds(r, blk), :], qt, preferred_element_type=F32) * SCALE + extra
               for r, (_, extra) in zip(rows, blocks)]
        ms = [jnp.max(st, axis=0, keepdims=True) for st in sts]
        ps = [jnp.exp(st - m) for st, m in zip(sts, ms)]
        ls = [jnp.sum(p, axis=0, keepdims=True) for p in ps]
        os_ = [jnp.dot(vt_s[:, pl.ds(r, blk)], p.astype(BF16), preferred_element_type=F32)
               for r, p in zip(rows, ps)]
        return list(zip(ms, ls, os_))

    def merge(parts):
        m = functools.reduce(jnp.maximum, [pm for pm, _, _ in parts])
        w = [jnp.exp(pm - m) for pm, _, _ in parts]
        l = functools.reduce(jnp.add, [wi * pl_ for wi, (_, pl_, _) in zip(w, parts)])
        o = functools.reduce(jnp.add, [wi * po for wi, (_, _, po) in zip(w, parts)])
        return m, l, o

    m0, l0, o0 = merge(partial_softmax([(i, bias_ref[0]),
                                        (jnp.maximum(i - 1, 0), bias_ref[1] + prev_row)]))
    m_s[...] = m0
    l_s[...] = l0
    acc_s[...] = o0

    def run_groups(first, count, width):
        def body(g, carry):
            j0 = first + g * width
            parts = partial_softmax([(j0 + u, far_s[pl.ds(j0 + u, 1), :]) for u in range(width)])
            m, l, o = merge([(m_s[...], l_s[...], acc_s[...])] + parts)
            m_s[...] = m
            l_s[...] = l
            acc_s[...] = o
            return carry

        lax.fori_loop(0, count, body, 0)

    n_far = jnp.maximum(i - 1, 0)
    n_wide = n_far // ATTN_WIDE_GROUP
    left = n_far - n_wide * ATTN_WIDE_GROUP
    run_groups(0, n_wide, ATTN_WIDE_GROUP)
    run_groups(n_wide * ATTN_WIDE_GROUP, (left + ATTN_NARROW_GROUP - 1) // ATTN_NARROW_GROUP,
               ATTN_NARROW_GROUP)

    o_ref[...] = (acc_s[...] / l_s[...]).T


def _prompt_attention(q, k, v, bias, rel_table, batch, seq):
    nb = seq // MOBA_BLOCK
    blk = MOBA_BLOCK
    qo = lambda b, h, i: (b * nb + i, h)
    kv = lambda b, h, i: (b, h)
    return pl.pallas_call(
        functools.partial(_attn_kernel, nb),
        out_shape=jax.ShapeDtypeStruct((batch * seq, ATTN_WIDTH), F32),
        grid=(batch, N_HEADS, nb),
        in_specs=[pl.BlockSpec(memory_space=pltpu.SMEM),
                  pl.BlockSpec((blk, HEAD_DIM), qo),
                  pl.BlockSpec((seq, HEAD_DIM), kv),
                  pl.BlockSpec((seq, HEAD_DIM), kv),
                  pl.BlockSpec((None, 2, blk, blk), lambda b, h, i: (h, 0, 0, 0))],
        out_specs=pl.BlockSpec((blk, HEAD_DIM), qo),
        scratch_shapes=[pltpu.VMEM((seq, HEAD_DIM), BF16), pltpu.VMEM((HEAD_DIM, seq), BF16),
                        pltpu.VMEM((nb, HEAD_DIM), F32), pltpu.VMEM((nb, blk), F32),
                        pltpu.VMEM((1, blk), F32), pltpu.VMEM((1, blk), F32),
                        pltpu.VMEM((HEAD_DIM, blk), F32)],
        compiler_params=_cparams(3), name="moba_prompt")(rel_table, q, k, v, bias)


PAGES_PER_STEP = 8


def _sgate_kernel(n_blk, pt_ref, q_ref, *rest):
    pages = rest[:PAGES_PER_STEP]
    g_ref, top_ref = rest[PAGES_PER_STEP:]
    s = pl.program_id(1)
    blocks_per_step = PAGES_PER_STEP // 2

    @pl.when(s == 0)
    def _():
        g_ref[...] = jnp.zeros(g_ref.shape, F32)

    q = q_ref[...]
    lane = lax.broadcasted_iota(I32, (N_HEADS, n_blk), 1)
    acc = g_ref[...]
    for p in range(blocks_per_step):
        bsum = jnp.sum(pages[2 * p][...], axis=0) + jnp.sum(pages[2 * p + 1][...], axis=0)
        gn = jnp.sum(q * (bsum * (1.0 / MOBA_BLOCK)), axis=1, keepdims=True)
        acc = jnp.where(lane == s * blocks_per_step + p, gn, acc)
    g_ref[...] = acc

    @pl.when(s == pl.num_programs(1) - 1)
    def _():
        g = acc
        out_lane = lax.broadcasted_iota(I32, (N_HEADS, LANES), 1)
        top = jnp.zeros((N_HEADS, LANES), I32)
        for t in range(MOBA_TOPK):
            m = jnp.max(g, axis=1, keepdims=True)
            idx = jnp.min(jnp.where(g == m, lane, n_blk), axis=1, keepdims=True)
            top = jnp.where(out_lane == t, idx, top)
            g = jnp.where(lane == idx, -jnp.inf, g)
        top_ref[...] = top


def _sample_gate(q3, cache_k, page_table):
    bs = q3.shape[0]
    n_pages = page_table.shape[1]
    n_blk = n_pages * PAGE_SIZE // MOBA_BLOCK
    assert n_pages % PAGES_PER_STEP == 0 and n_blk >= MOBA_TOPK
    page_shape = (None, PAGE_SIZE, N_HEADS, HEAD_DIM)

    def page_spec(r):
        return pl.BlockSpec(page_shape, lambda b, s, pt: (pt[b, s * PAGES_PER_STEP + r], 0, 0, 0))

    per_b = lambda b, s, pt: (b, 0, 0)
    grid_spec = pltpu.PrefetchScalarGridSpec(
        num_scalar_prefetch=1, grid=(bs, n_pages // PAGES_PER_STEP),
        in_specs=[pl.BlockSpec((None, N_HEADS, HEAD_DIM), per_b)]
        + [page_spec(r) for r in range(PAGES_PER_STEP)],
        out_specs=(pl.BlockSpec((None, N_HEADS, n_blk), per_b),
                   pl.BlockSpec((None, N_HEADS, LANES), per_b)))
    _, top = pl.pallas_call(
        functools.partial(_sgate_kernel, n_blk),
        out_shape=(jax.ShapeDtypeStruct((bs, N_HEADS, n_blk), F32),
                   jax.ShapeDtypeStruct((bs, N_HEADS, LANES), I32)),
        grid_spec=grid_spec, compiler_params=_cparams(2), name="moba_sample_gate")(
            page_table, q3, *([cache_k] * PAGES_PER_STEP))
    return top[:, :, :MOBA_TOPK].reshape(bs, N_HEADS * MOBA_TOPK)


def _sattn_kernel(n_blk, pt_ref, top_ref, tab_ref, q_ref, kn_ref, vn_ref, b_last_ref,
                  k0_ref, k1_ref, v0_ref, v1_ref, o_ref, m_s, l_s, acc_s):
    b = pl.program_id(0)
    h = pl.program_id(1)
    t = pl.program_id(2)
    hm = lax.broadcasted_iota(I32, (N_HEADS, HEAD_DIM), 0) == h

    def head_row(x):
        return jnp.sum(jnp.where(hm[:, :1], x, 0.0), axis=0, keepdims=True)

    q = q_ref[...]
    qh = head_row(q)

    @pl.when(t == 0)
    def _():
        s_self = jnp.sum(qh * head_row(kn_ref[...]), axis=1, keepdims=True) * SCALE + tab_ref[0, h]
        m_s[...] = jnp.broadcast_to(s_self, m_s.shape)
        l_s[...] = jnp.ones(l_s.shape, F32)
        acc_s[...] = head_row(vn_ref[...])

    blk = top_ref[b, h * MOBA_TOPK + t]
    q8 = jnp.where(hm, q, 0.0).astype(BF16)
    far_bias = tab_ref[REL_BUCKETS - 1, h]
    b_last = head_row(b_last_ref[...])
    for r, (kr, vr) in enumerate(((k0_ref, v0_ref), (k1_ref, v1_ref))):
        kh = kr[:, h, :].astype(BF16)
        vh = vr[:, h, :].astype(BF16)
        s8 = lax.dot_general(q8, kh, _NT, preferred_element_type=F32)
        s = head_row(s8) * SCALE
        s = s + jnp.where(blk == n_blk - 1, b_last[:, r * PAGE_SIZE:(r + 1) * PAGE_SIZE], far_bias)
        m_old = m_s[...]
        m_new = jnp.maximum(m_old, jnp.max(s, axis=1, keepdims=True))
        alpha = jnp.exp(m_old - m_new)
        p = jnp.exp(s - m_new)
        l_s[...] = alpha * l_s[...] + jnp.sum(p, axis=1, keepdims=True)
        p8 = jnp.broadcast_to(p, (N_HEADS, PAGE_SIZE)).astype(BF16)
        pv = jnp.dot(p8, vh, preferred_element_type=F32)
        acc_s[...] = alpha * acc_s[...] + pv[:1, :]
        m_s[...] = m_new

    @pl.when(t == MOBA_TOPK - 1)
    def _():
        o_ref[...] = acc_s[...] / l_s[...]


def _sample_attention(q3, kn3, vn3, cache_k, cache_v, page_table, top, rel_table, bias_last):
    bs = q3.shape[0]
    n_blk = page_table.shape[1] * PAGE_SIZE // MOBA_BLOCK
    page_shape = (None, PAGE_SIZE, N_HEADS, HEAD_DIM)

    def page_spec(r):
        return pl.BlockSpec(
            page_shape,
            lambda b, h, t, pt, tp: (pt[b, 2 * tp[b, h * MOBA_TOPK + t] + r], 0, 0, 0))

    per_b = pl.BlockSpec((None, N_HEADS, HEAD_DIM), lambda b, h, t, pt, tp: (b, 0, 0))
    grid_spec = pltpu.PrefetchScalarGridSpec(
        num_scalar_prefetch=2, grid=(bs, N_HEADS, MOBA_TOPK),
        in_specs=[pl.BlockSpec(memory_space=pltpu.SMEM), per_b, per_b, per_b,
                  pl.BlockSpec((N_HEADS, MOBA_BLOCK), lambda b, h, t, pt, tp: (0, 0)),
                  page_spec(0), page_spec(1), page_spec(0), page_spec(1)],
        out_specs=pl.BlockSpec((None, None, 1, HEAD_DIM), lambda b, h, t, pt, tp: (b, h, 0, 0)),
        scratch_shapes=[pltpu.VMEM((1, LANES), F32), pltpu.VMEM((1, LANES), F32),
                        pltpu.VMEM((1, HEAD_DIM), F32)])
    out = pl.pallas_call(
        functools.partial(_sattn_kernel, n_blk),
        out_shape=jax.ShapeDtypeStruct((bs, N_HEADS, 1, HEAD_DIM), F32),
        grid_spec=grid_spec, compiler_params=_cparams(3), name="moba_sample")(
            page_table, top, rel_table, q3, kn3, vn3, bias_last, cache_k, cache_k, cache_v, cache_v)
    return out.reshape(bs, ATTN_WIDTH)


CONV_HALO = 32
CONV_ROWS = 32


def _ln_swish(y, g, b):
    mu = jnp.mean(y, axis=-1, keepdims=True)
    var = jnp.mean(jnp.square(y - mu), axis=-1, keepdims=True)
    yn = (y - mu) * lax.rsqrt(var + EPS) * g + b
    return yn * jax.nn.sigmoid(yn)


def _conv_kernel(tt, cur_ref, halo_ref, w_ref, b_ref, lg_ref, lb_ref, y_ref, ext_s, yc_s):
    t = pl.program_id(1)
    ch = cur_ref.shape[1]
    ext_s[CONV_HALO:, :] = cur_ref[...]

    @pl.when(t == 0)
    def _():
        ext_s[:CONV_HALO, :] = jnp.zeros((CONV_HALO, ch), F32)

    @pl.when(t > 0)
    def _():
        ext_s[:CONV_HALO, :] = halo_ref[...]

    lead = CONV_HALO - (CONV_WIDTH - 1)
    rc = CONV_ROWS

    def chunk(r, carry):
        r0 = pl.multiple_of(r * rc, rc)
        for lt in range(ch // LANES):
            ls = slice(lt * LANES, (lt + 1) * LANES)
            win = ext_s[pl.ds(r0, rc + CONV_HALO), ls]
            acc = jnp.zeros((rc, LANES), F32)
            for sh in range(SUBLANES):
                ws = win if sh == 0 else win[sh:sh + rc + CONV_HALO - SUBLANES]
                for a in range(CONV_HALO // SUBLANES + 1):
                    k = SUBLANES * a + sh - lead
                    if 0 <= k < CONV_WIDTH and SUBLANES * a + rc <= ws.shape[0]:
                        acc = acc + w_ref[k:k + 1, ls] * ws[SUBLANES * a:SUBLANES * a + rc]
            yc_s[pl.ds(r0, rc), ls] = acc + b_ref[:, ls]
        return carry

    lax.fori_loop(0, tt // rc, chunk, 0)
    y_ref[...] = _ln_swish(yc_s[...], lg_ref[...], lb_ref[...])


def _conv_prompt(u, w_dw, b_dw, ln_g, ln_b, batch, seq):
    ch = u.shape[1]
    tt = min(seq, 512)
    nt = seq // tt
    hpt = tt // CONV_HALO
    vec = lambda x: x.reshape(1, ch)
    return pl.pallas_call(
        functools.partial(_conv_kernel, tt),
        out_shape=jax.ShapeDtypeStruct((batch * seq, ch), F32), grid=(batch, nt),
        in_specs=[pl.BlockSpec((tt, ch), lambda b, t: (b * nt + t, 0)),
                  pl.BlockSpec((CONV_HALO, ch), lambda b, t: (jnp.maximum((b * nt + t) * hpt - 1, 0), 0)),
                  _resident(w_dw.shape), _resident((1, ch)), _resident((1, ch)), _resident((1, ch))],
        out_specs=pl.BlockSpec((tt, ch), lambda b, t: (b * nt + t, 0)),
        scratch_shapes=[pltpu.VMEM((tt + CONV_HALO, ch), F32), pltpu.VMEM((tt, ch), F32)],
        compiler_params=_cparams(2), name="conv_prompt")(u, u, w_dw, vec(b_dw), vec(ln_g), vec(ln_b))


def _conv_step_kernel(st_ref, u_ref, w_ref, b_ref, lg_ref, lb_ref, y_ref, ns_ref):
    hist = CONV_WIDTH - 1
    u = u_ref[...]
    acc = w_ref[hist:hist + 1, :] * u
    for k in range(hist):
        acc = acc + w_ref[k:k + 1, :] * st_ref[:, k, :]
    y_ref[...] = _ln_swish(acc + b_ref[...], lg_ref[...], lb_ref[...])
    for k in range(hist - 1):
        ns_ref[:, k, :] = st_ref[:, k + 1, :]
    ns_ref[:, hist - 1, :] = u


def _conv_step(state, u, w_dw, b_dw, ln_g, ln_b):
    bs, hist, ch = state.shape
    vec = lambda x: x.reshape(1, ch)
    return pl.pallas_call(
        _conv_step_kernel,
        out_shape=(jax.ShapeDtypeStruct((bs, ch), F32), jax.ShapeDtypeStruct((bs, hist, ch), F32)),
        compiler_params=pltpu.CompilerParams(vmem_limit_bytes=VMEM_LIMIT_BYTES),
        name="conv_step")(state, u, w_dw, vec(b_dw), vec(ln_g), vec(ln_b))


def _mix_kernel(attn_ref, conv_ref, x_ref, wo_ref, g_ref, wq_ref, h_ref, xn_ref, pq_ref):
    mix = jnp.concatenate([attn_ref[...], conv_ref[...]], axis=1).astype(BF16)
    h = x_ref[...] + jnp.dot(mix, wo_ref[...], preferred_element_type=F32)
    h_ref[...] = h
    xn = _rmsnorm(h, g_ref[...]).astype(BF16)
    xn_ref[...] = xn
    pq_ref[...] = jnp.dot(xn, wq_ref[...], preferred_element_type=F32)


def _mix_out(attn, conv, x, wo_bf16, g, wq_bf16):
    m, d = x.shape
    tm = min(m, 256)
    row = lambda i: (i, 0)
    half = attn.shape[1]
    return pl.pallas_call(
        _mix_kernel,
        out_shape=(jax.ShapeDtypeStruct((m, d), F32), jax.ShapeDtypeStruct((m, d), BF16),
                   jax.ShapeDtypeStruct((m, wq_bf16.shape[1]), F32)),
        grid=(m // tm,),
        in_specs=[pl.BlockSpec((tm, half), row), pl.BlockSpec((tm, half), row), pl.BlockSpec((tm, d), row),
                  _resident(wo_bf16.shape), _resident((1, d)), _resident(wq_bf16.shape)],
        out_specs=(pl.BlockSpec((tm, d), row), pl.BlockSpec((tm, d), row),
                   pl.BlockSpec((tm, wq_bf16.shape[1]), row)),
        compiler_params=_cparams(1), name="mix_out")(attn, conv, x, wo_bf16, g.reshape(1, d), wq_bf16)


ROUTE_TOKENS = LANES
_BIG = 1 << 20


def _candidate_layout():
    k = PEER_TOPK
    groups, flat = [], []
    for a in range(k // 2):
        nb = k // (a + 1)
        rows = -(-nb // SUBLANES) * SUBLANES
        groups.append((a, 1, rows))
        flat += [a * k + b if b < nb else _BIG for b in range(rows)]
    groups.append((k // 2, k // 2, 1))
    flat += [a * k for a in range(k // 2, k)]
    return groups, np.asarray(flat, np.int32)


def _route_kernel(pq_ref, sk_ref, flat_ref, e_ref, g_ref, et_s, gt_s):
    k = PEER_TOPK
    nk = PEER_NKEYS
    tn = ROUTE_TOKENS
    row = lax.broadcasted_iota(I32, (nk, tn), 0)
    groups, _ = _candidate_layout()
    flat = flat_ref[...]

    def half_topk(h, c):
        off = pl.multiple_of((h * 2 + c) * nk, nk)
        s = lax.dot_general(sk_ref[h, c], pq_ref[:, pl.ds(off, nk)], _NT,
                            precision=lax.Precision.HIGHEST, preferred_element_type=F32)
        ts, ti = [], []
        for _ in range(k):
            m = jnp.max(s, axis=0, keepdims=True)
            idx = jnp.min(jnp.where(s == m, row, nk), axis=0, keepdims=True)
            ts.append(m)
            ti.append(idx)
            s = jnp.where(row == idx, -jnp.inf, s)
        return jnp.concatenate(ts, axis=0), jnp.concatenate(ti, axis=0)

    def head(h, carry):
        s0, i0 = half_topk(h, 0)
        s1, i1 = half_topk(h, 1)
        cs, ce = [], []
        for a0, na, nb in groups:
            if na == 1:
                cs.append(s0[a0:a0 + 1] + s1[:nb])
                ce.append(i0[a0:a0 + 1] * nk + i1[:nb])
            else:
                cs.append(s0[a0:a0 + na] + s1[:1])
                ce.append(i0[a0:a0 + na] * nk + i1[:1])
        cand = jnp.where(flat < _BIG, jnp.concatenate(cs, axis=0), -jnp.inf)
        cexp = jnp.concatenate(ce, axis=0)
        bs, be = [], []
        for _ in range(k):
            m = jnp.max(cand, axis=0, keepdims=True)
            idx = jnp.min(jnp.where(cand == m, flat, _BIG), axis=0, keepdims=True)
            hit = flat == idx
            bs.append(m)
            be.append(jnp.max(jnp.where(hit, cexp, -1), axis=0, keepdims=True))
            cand = jnp.where(hit, -jnp.inf, cand)
        best = jnp.concatenate(bs, axis=0)
        p = jnp.exp(best - best[:1])
        r = pl.multiple_of(h * k, k)
        gt_s[pl.ds(r, k), :] = p / jnp.sum(p, axis=0, keepdims=True)
        et_s[pl.ds(r, k), :] = jnp.concatenate(be, axis=0)
        return carry

    lax.fori_loop(0, PEER_HEADS, head, 0, unroll=2)
    e_ref[...] = et_s[...].T
    g_ref[...] = gt_s[...].T


def _route(pq, sub_keys):
    n, qd = pq.shape
    tn = ROUTE_TOKENS
    _, flat = _candidate_layout()
    flat = jnp.asarray(np.broadcast_to(flat[:, None], (flat.shape[0], tn)))
    row = lambda i: (i, 0)
    return pl.pallas_call(
        _route_kernel,
        out_shape=(jax.ShapeDtypeStruct((n, PEER_SLOTS), I32), jax.ShapeDtypeStruct((n, PEER_SLOTS), F32)),
        grid=(n // tn,),
        in_specs=[pl.BlockSpec((tn, qd), row), _resident(sub_keys.shape), _resident(flat.shape)],
        out_specs=(pl.BlockSpec((tn, PEER_SLOTS), row), pl.BlockSpec((tn, PEER_SLOTS), row)),
        scratch_shapes=[pltpu.VMEM((PEER_SLOTS, tn), I32), pltpu.VMEM((PEER_SLOTS, tn), F32)],
        compiler_params=_cparams(1), name="peer_route")(pq, sub_keys, flat)


EXPERT_CHUNK = 1024
_NKEYS_BITS = PEER_NKEYS.bit_length() - 1
assert 1 << _NKEYS_BITS == PEER_NKEYS


def _split_expert(e):
    return lax.shift_right_logical(e, _NKEYS_BITS), e & (PEER_NKEYS - 1)


def _peer_up_kernel(xn_ref, e_ref, u_ref, hs_ref):
    c = pl.program_id(1)
    groups = EXPERT_CHUNK // PEER_NKEYS

    @pl.when(c == 0)
    def _():
        hs_ref[...] = jnp.zeros(hs_ref.shape, F32)

    hd = lax.dot_general(xn_ref[...], u_ref[...], _NT, preferred_element_type=F32)
    e = e_ref[...]
    i1, i2 = _split_expert(e)
    acc = hs_ref[...]
    for g in range(groups):
        got = jnp.take_along_axis(hd[:, g * PEER_NKEYS:(g + 1) * PEER_NKEYS], i2, axis=1)
        acc = jnp.where(i1 == c * groups + g, got, acc)
    hs_ref[...] = acc


def _peer_up(xn_bf16, e, u_bf16):
    n, d = xn_bf16.shape
    n_exp = u_bf16.shape[0]
    tn = 1024 if n % 1024 == 0 else (256 if n % 256 == 0 else n)
    return pl.pallas_call(
        _peer_up_kernel, out_shape=jax.ShapeDtypeStruct((n, PEER_SLOTS), F32),
        grid=(n // tn, n_exp // EXPERT_CHUNK),
        in_specs=[pl.BlockSpec((tn, d), lambda i, c: (i, 0)),
                  pl.BlockSpec((tn, PEER_SLOTS), lambda i, c: (i, 0)),
                  pl.BlockSpec((EXPERT_CHUNK, d), lambda i, c: (c, 0))],
        out_specs=pl.BlockSpec((tn, PEER_SLOTS), lambda i, c: (i, 0)),
        compiler_params=_cparams(2), name="peer_up")(xn_bf16, e, u_bf16)


TOKEN_UNROLL = 8


def _peer_down_kernel(tn, e_ref, gate_ref, hs_ref, v_ref, h_ref, gf_ref, y_ref, act_s, a_s, acc_s):
    c = pl.program_id(1)
    nk = PEER_NKEYS
    groups = EXPERT_CHUNK // nk

    @pl.when(c == 0)
    def _():
        acc_s[...] = jnp.zeros(acc_s.shape, F32)
        hv = hs_ref[...]
        gelu = 0.5 * hv * (1.0 + lax.erf(hv * math.sqrt(0.5)))
        act_s[...] = gate_ref[...] * gelu
        sub = lax.broadcasted_iota(I32, (nk, PEER_SLOTS), 0)

        def token(n, carry):
            e = e_ref[pl.ds(n, 1), :]
            a = act_s[pl.ds(n, 1), :]
            i1, i2 = _split_expert(e)
            pt = jnp.where(sub == i1, a, 0.0).astype(BF16)
            qt = jnp.where(sub == i2, 1.0, 0.0).astype(BF16)
            a_s[pl.ds(pl.multiple_of(n * nk, nk), nk), :] = lax.dot_general(
                pt, qt, _NT, preferred_element_type=F32)
            return carry

        lax.fori_loop(0, tn, token, 0, unroll=TOKEN_UNROLL)

    lhs = jnp.concatenate(
        [a_s[pl.ds(c * groups + g, tn, stride=nk), :].astype(BF16) for g in range(groups)], axis=1)
    acc_s[...] += jnp.dot(lhs, v_ref[...], preferred_element_type=F32)

    @pl.when(c == pl.num_programs(1) - 1)
    def _():
        y_ref[...] = _rmsnorm(h_ref[...] + acc_s[...], gf_ref[...])


def _peer_down(e, gate, hs, v_bf16, h, g_final):
    n, d = h.shape
    n_exp = v_bf16.shape[0]
    tn = 256 if n % 256 == 0 else n
    tok = lambda i, c: (i, 0)
    return pl.pallas_call(
        functools.partial(_peer_down_kernel, tn),
        out_shape=jax.ShapeDtypeStruct((n, d), F32), grid=(n // tn, n_exp // EXPERT_CHUNK),
        in_specs=[pl.BlockSpec((tn, PEER_SLOTS), tok), pl.BlockSpec((tn, PEER_SLOTS), tok),
                  pl.BlockSpec((tn, PEER_SLOTS), tok),
                  pl.BlockSpec((EXPERT_CHUNK, d), lambda i, c: (c, 0)),
                  pl.BlockSpec((tn, d), tok), _resident((1, d))],
        out_specs=pl.BlockSpec((tn, d), tok),
        scratch_shapes=[pltpu.VMEM((tn, PEER_SLOTS), F32), pltpu.VMEM((tn * PEER_NKEYS, PEER_NKEYS), F32),
                        pltpu.VMEM((tn, d), F32)],
        compiler_params=_cparams(2), name="peer_down")(e, gate, hs, v_bf16, h, g_final.reshape(1, d))


def _peer_and_final_norm(h, xn_bf16, pq, sub_keys, u_bf16, v_bf16, g_final):
    n = h.shape[0]
    n_pad = -(-n // ROUTE_TOKENS) * ROUTE_TOKENS
    pq_pad = pq if n_pad == n else jnp.pad(pq, ((0, n_pad - n), (0, 0)))
    e, gate = _route(pq_pad, sub_keys)
    e, gate = e[:n], gate[:n]
    hs = _peer_up(xn_bf16, e, u_bf16)
    return _peer_down(e, gate, hs, v_bf16, h, g_final)


def kernel(x_prompt, x_sample, cache_k, cache_v, state_conv, page_table, rel_bias_table, norm_mix_g, w_in,
           w_dw, b_dw, conv_ln_g, conv_ln_b, w_out, norm_ffn_g, peer_wq, peer_sub_keys, peer_u, peer_v,
           final_norm_g):
    batch, seq, d = x_prompt.shape
    bs, s_new, _ = x_sample.shape
    depth = w_in.shape[0]
    assert s_new == 1 and depth == 1 and seq % MOBA_BLOCK == 0 and seq // MOBA_BLOCK >= MOBA_TOPK
    hist = CONV_WIDTH - 1
    blk = MOBA_BLOCK

    dist = np.arange(blk)[None, :] - np.arange(blk)[:, None]
    bkt_prompt = jnp.concatenate([_rel_bucket(dist), _rel_bucket(dist + blk)], axis=0)
    bias_prompt = _bias_lookup(rel_bias_table, bkt_prompt).reshape(N_HEADS, 2, blk, blk)
    bkt_last = jnp.broadcast_to(_rel_bucket(blk - np.arange(blk))[None, :], (SUBLANES, blk))
    bias_last = _bias_lookup(rel_bias_table, bkt_last)[:, 0, :]

    hp = x_prompt.reshape(batch * seq, d)
    hs_ = x_sample.reshape(bs, d)
    l = 0
    w_in_b = w_in[l].astype(BF16)
    w_out_b = w_out[l].astype(BF16)
    wq_b = peer_wq[l].astype(BF16)
    u_b = peer_u[l].astype(BF16)
    v_b = peer_v[l].astype(BF16)

    qp, kp, vp, up = _proj(hp, norm_mix_g[l], w_in_b)
    attn_p = _prompt_attention(qp, kp, vp, bias_prompt, rel_bias_table, batch, seq)
    conv_p = _conv_prompt(up, w_dw[l], b_dw[l], conv_ln_g[l], conv_ln_b[l], batch, seq)
    h_p, xn_p, pq_p = _mix_out(attn_p, conv_p, hp, w_out_b, norm_ffn_g[l], wq_b)
    y_p = _peer_and_final_norm(h_p, xn_p, pq_p, peer_sub_keys[l], u_b, v_b, final_norm_g)

    qs, ks, vs, us = _proj(hs_, norm_mix_g[l], w_in_b)
    q3 = qs.reshape(bs, N_HEADS, HEAD_DIM)
    top = _sample_gate(q3, cache_k[l], page_table)
    attn_s = _sample_attention(q3, ks.reshape(q3.shape), vs.reshape(q3.shape), cache_k[l], cache_v[l],
                               page_table, top, rel_bias_table, bias_last)
    conv_s, state_new = _conv_step(state_conv[l], us, w_dw[l], b_dw[l], conv_ln_g[l], conv_ln_b[l])
    h_s, xn_s, pq_s = _mix_out(attn_s, conv_s, hs_, w_out_b, norm_ffn_g[l], wq_b)
    y_s = _peer_and_final_norm(h_s, xn_s, pq_s, peer_sub_keys[l], u_b, v_b, final_norm_g)

    kv_p = (1, batch, seq, N_HEADS, HEAD_DIM)
    kv_s = (1, bs, 1, N_HEADS, HEAD_DIM)
    conv_prompt_new = up.reshape(batch, seq, -1)[:, seq - hist:, :][None]
    return (y_p.reshape(batch, seq, d), y_s.reshape(bs, 1, d),
            kp.reshape(kv_p), vp.reshape(kv_p), conv_prompt_new,
            ks.reshape(kv_s), vs.reshape(kv_s), state_new[None])
```

```python
import functools
import math

import numpy as np
import jax
import jax.numpy as jnp
from jax import lax
from jax.experimental import pallas as pl
from jax.experimental.pallas import tpu as pltpu

F32, BF16, I32 = jnp.float32, jnp.bfloat16, jnp.int32

N_HEADS = 8
HEAD_DIM = 128
ATTN_WIDTH = N_HEADS * HEAD_DIM
MOBA_BLOCK = 256
MOBA_TOPK = 3
PAGE_SIZE = 128
REL_BUCKETS = 32
REL_MAX_DIST = 128
CONV_WIDTH = 31
PEER_HEADS = 8
PEER_NKEYS = 128
PEER_TOPK = 16
PEER_SLOTS = PEER_HEADS * PEER_TOPK
EPS = 1e-6
NEG = -1e30
SCALE = HEAD_DIM ** -0.5

LANES = 128
SUBLANES = 8
VMEM_LIMIT_BYTES = 56 * 1024 * 1024
DOWN_VMEM_LIMIT_BYTES = 60 * 1024 * 1024

_NT = (((1,), (1,)), ((), ()))


def _cparams(n_axes, vmem_limit_bytes=VMEM_LIMIT_BYTES):
    return pltpu.CompilerParams(dimension_semantics=("arbitrary",) * n_axes,
                                vmem_limit_bytes=vmem_limit_bytes)


def _resident(shape):
    nd = len(shape)
    return pl.BlockSpec(shape, lambda *_: (0,) * nd, pipeline_mode=pl.Buffered(1))


def _rmsnorm(x, g):
    return x * lax.rsqrt(jnp.mean(x * x, axis=-1, keepdims=True) + EPS) * g


def _proj_kernel(x_ref, g_ref, w_ref, q_ref, k_ref, v_ref, u_ref):
    xn = _rmsnorm(x_ref[...], g_ref[...]).astype(BF16)
    c = ATTN_WIDTH

    def mm(j):
        return jnp.dot(xn, w_ref[:, j * c:(j + 1) * c], preferred_element_type=F32)

    q_ref[...] = mm(0)
    k_ref[...] = mm(1)
    v_ref[...] = mm(2)
    u_ref[...] = mm(3) * jax.nn.sigmoid(mm(4))


def _proj(x, g, w_bf16):
    m, d = x.shape
    tm = min(m, 256)
    row = lambda i: (i, 0)
    out = jax.ShapeDtypeStruct((m, ATTN_WIDTH), F32)
    return pl.pallas_call(
        _proj_kernel, out_shape=(out,) * 4, grid=(m // tm,),
        in_specs=[pl.BlockSpec((tm, d), row), _resident((1, d)), _resident(w_bf16.shape)],
        out_specs=(pl.BlockSpec((tm, ATTN_WIDTH), row),) * 4,
        compiler_params=_cparams(1), name="proj")(x, g.reshape(1, d), w_bf16)


MASKED_BUCKET = REL_BUCKETS


def _rel_bucket(dist):
    dist = jnp.asarray(dist, I32)
    n = jnp.maximum(dist, 0)
    max_exact = REL_BUCKETS // 2
    nf = jnp.maximum(n, 1).astype(F32)
    large = max_exact + (jnp.log(nf / max_exact) / math.log(REL_MAX_DIST / max_exact)
                         * (REL_BUCKETS - max_exact)).astype(I32)
    bucket = jnp.where(n < max_exact, n, jnp.minimum(large, REL_BUCKETS - 1))
    return jnp.where(dist < 0, MASKED_BUCKET, bucket)


def _bias_kernel(tab_ref, bkt_ref, o_ref):
    h = pl.program_id(0)
    bkt = bkt_ref[...]
    acc = jnp.full(bkt.shape, NEG, F32)
    for b in range(REL_BUCKETS):
        acc = jnp.where(bkt == b, tab_ref[b, h], acc)
    o_ref[...] = acc


def _bias_lookup(rel_table, bkt):
    r, c = bkt.shape
    return pl.pallas_call(
        _bias_kernel, out_shape=jax.ShapeDtypeStruct((N_HEADS, r, c), F32), grid=(N_HEADS,),
        in_specs=[pl.BlockSpec(memory_space=pltpu.SMEM), _resident((r, c))],
        out_specs=pl.BlockSpec((None, r, c), lambda h: (h, 0, 0)),
        compiler_params=_cparams(1), name="rel_bias")(rel_table, bkt)


ATTN_WIDE_GROUP = 4
ATTN_NARROW_GROUP = 2


def _attn_kernel(nb, tab_ref, q_ref, k_ref, v_ref, bias_ref, o_ref,
                 kb_s, vt_s, km_s, far_s, m_s, l_s, acc_s):
    h = pl.program_id(1)
    i = pl.program_id(2)
    blk = MOBA_BLOCK

    @pl.when(i == 0)
    def _():
        kb_s[...] = k_ref[...].astype(BF16)
        for j in range(nb):
            rows = slice(j * blk, (j + 1) * blk)
            km_s[j:j + 1, :] = jnp.mean(k_ref[rows, :], axis=0, keepdims=True)
            vt_s[:, rows] = v_ref[rows, :].T.astype(BF16)

    q = q_ref[...]
    qt = q.T.astype(BF16)

    gate = lax.dot_general(km_s[...], q, _NT, precision=lax.Precision.HIGHEST,
                           preferred_element_type=F32)
    sub = lax.broadcasted_iota(I32, (nb, blk), 0)
    gate = jnp.where(sub < i, gate, NEG)
    rank = jnp.zeros((nb, blk), I32)
    for jp in range(nb):
        gj = gate[jp:jp + 1, :]
        rank = rank + jnp.where(gj == gate, jnp.where(jp < sub, 1, 0),
                                jnp.where(gj > gate, 1, 0))
    selneg = jnp.where((rank < MOBA_TOPK) & (sub < i), 0.0, NEG)
    far_s[...] = jnp.where(sub < i - 1, selneg + tab_ref[REL_BUCKETS - 1, h], NEG)
    prev_row = jnp.sum(jnp.where(sub == i - 1, selneg, 0.0), axis=0, keepdims=True)
    prev_row = jnp.where(i >= 1, prev_row, NEG)

    def partial_softmax(blocks):
        rows = [pl.multiple_of(j * blk, blk) for j, _ in blocks]
        sts = [jnp.dot(kb_s[pl.ds(r, blk), :], qt, preferred_element_type=F32) * SCALE + extra
               for r, (_, extra) in zip(rows, blocks)]
        ms = [jnp.max(st, axis=0, keepdims=True) for st in sts]
        ps = [jnp.exp(st - m) for st, m in zip(sts, ms)]
        ls = [jnp.sum(p, axis=0, keepdims=True) for p in ps]
        os_ = [jnp.dot(vt_s[:, pl.ds(r, blk)], p.astype(BF16), preferred_element_type=F32)
               for r, p in zip(rows, ps)]
        return list(zip(ms, ls, os_))

    def merge(parts):
        m = functools.reduce(jnp.maximum, [pm for pm, _, _ in parts])
        w = [jnp.exp(pm - m) for pm, _, _ in parts]
        l = functools.reduce(jnp.add, [wi * pl_ for wi, (_, pl_, _) in zip(w, parts)])
        o = functools.reduce(jnp.add, [wi * po for wi, (_, _, po) in zip(w, parts)])
        return m, l, o

    m0, l0, o0 = merge(partial_softmax([(i, bias_ref[0]),
                                        (jnp.maximum(i - 1, 0), bias_ref[1] + prev_row)]))
    m_s[...] = m0
    l_s[...] = l0
    acc_s[...] = o0

    def run_groups(first, count, width):
        def body(g, carry):
            j0 = first + g * width
            parts = partial_softmax([(j0 + u, far_s[pl.ds(j0 + u, 1), :]) for u in range(width)])
            m, l, o = merge([(m_s[...], l_s[...], acc_s[...])] + parts)
            m_s[...] = m
            l_s[...] = l
            acc_s[...] = o
            return carry

        lax.fori_loop(0, count, body, 0)

    n_far = jnp.maximum(i - 1, 0)
    n_wide = n_far // ATTN_WIDE_GROUP
    left = n_far - n_wide * ATTN_WIDE_GROUP
    run_groups(0, n_wide, ATTN_WIDE_GROUP)
    run_groups(n_wide * ATTN_WIDE_GROUP, (left + ATTN_NARROW_GROUP - 1) // ATTN_NARROW_GROUP,
               ATTN_NARROW_GROUP)

    o_ref[...] = (acc_s[...] / l_s[...]).T


def _prompt_attention(q, k, v, bias, rel_table, batch, seq):
    nb = seq // MOBA_BLOCK
    blk = MOBA_BLOCK
    qo = lambda b, h, i: (b * nb + i, h)
    kv = lambda b, h, i: (b, h)
    return pl.pallas_call(
        functools.partial(_attn_kernel, nb),
        out_shape=jax.ShapeDtypeStruct((batch * seq, ATTN_WIDTH), F32),
        grid=(batch, N_HEADS, nb),
        in_specs=[pl.BlockSpec(memory_space=pltpu.SMEM),
                  pl.BlockSpec((blk, HEAD_DIM), qo),
                  pl.BlockSpec((seq, HEAD_DIM), kv),
                  pl.BlockSpec((seq, HEAD_DIM), kv),
                  pl.BlockSpec((None, 2, blk, blk), lambda b, h, i: (h, 0, 0, 0))],
        out_specs=pl.BlockSpec((blk, HEAD_DIM), qo),
        scratch_shapes=[pltpu.VMEM((seq, HEAD_DIM), BF16), pltpu.VMEM((HEAD_DIM, seq), BF16),
                        pltpu.VMEM((nb, HEAD_DIM), F32), pltpu.VMEM((nb, blk), F32),
                        pltpu.VMEM((1, blk), F32), pltpu.VMEM((1, blk), F32),
                        pltpu.VMEM((HEAD_DIM, blk), F32)],
        compiler_params=_cparams(3), name="moba_prompt")(rel_table, q, k, v, bias)


PAGES_PER_STEP = 16


def _sgate_kernel(n_blk, pt_ref, q_ref, *rest):
    pages = rest[:PAGES_PER_STEP]
    g_ref, top_ref = rest[PAGES_PER_STEP:]
    s = pl.program_id(1)
    blocks_per_step = PAGES_PER_STEP // 2

    @pl.when(s == 0)
    def _():
        g_ref[...] = jnp.zeros(g_ref.shape, F32)

    q = q_ref[...]
    lane = lax.broadcasted_iota(I32, (N_HEADS, n_blk), 1)
    acc = g_ref[...]
    for p in range(blocks_per_step):
        bsum = jnp.sum(pages[2 * p][...], axis=0) + jnp.sum(pages[2 * p + 1][...], axis=0)
        gn = jnp.sum(q * (bsum * (1.0 / MOBA_BLOCK)), axis=1, keepdims=True)
        acc = jnp.where(lane == s * blocks_per_step + p, gn, acc)
    g_ref[...] = acc

    @pl.when(s == pl.num_programs(1) - 1)
    def _():
        g = acc
        out_lane = lax.broadcasted_iota(I32, (N_HEADS, LANES), 1)
        top = jnp.zeros((N_HEADS, LANES), I32)
        for t in range(MOBA_TOPK):
            m = jnp.max(g, axis=1, keepdims=True)
            idx = jnp.min(jnp.where(g == m, lane, n_blk), axis=1, keepdims=True)
            top = jnp.where(out_lane == t, idx, top)
            g = jnp.where(lane == idx, -jnp.inf, g)
        top_ref[...] = top


def _sample_gate(q3, cache_k, page_table):
    bs = q3.shape[0]
    n_pages = page_table.shape[1]
    n_blk = n_pages * PAGE_SIZE // MOBA_BLOCK
    assert n_pages % PAGES_PER_STEP == 0 and n_blk >= MOBA_TOPK
    page_shape = (None, PAGE_SIZE, N_HEADS, HEAD_DIM)

    def page_spec(r):
        return pl.BlockSpec(page_shape, lambda b, s, pt: (pt[b, s * PAGES_PER_STEP + r], 0, 0, 0))

    per_b = lambda b, s, pt: (b, 0, 0)
    grid_spec = pltpu.PrefetchScalarGridSpec(
        num_scalar_prefetch=1, grid=(bs, n_pages // PAGES_PER_STEP),
        in_specs=[pl.BlockSpec((None, N_HEADS, HEAD_DIM), per_b)]
        + [page_spec(r) for r in range(PAGES_PER_STEP)],
        out_specs=(pl.BlockSpec((None, N_HEADS, n_blk), per_b),
                   pl.BlockSpec((None, N_HEADS, LANES), per_b)))
    _, top = pl.pallas_call(
        functools.partial(_sgate_kernel, n_blk),
        out_shape=(jax.ShapeDtypeStruct((bs, N_HEADS, n_blk), F32),
                   jax.ShapeDtypeStruct((bs, N_HEADS, LANES), I32)),
        grid_spec=grid_spec, compiler_params=_cparams(2), name="moba_sample_gate")(
            page_table, q3, *([cache_k] * PAGES_PER_STEP))
    return top[:, :, :MOBA_TOPK].reshape(bs, N_HEADS * MOBA_TOPK)


PAGES_PER_BLOCK = MOBA_BLOCK // PAGE_SIZE
PAGES_PER_HEAD = MOBA_TOPK * PAGES_PER_BLOCK
PAGES_PER_SEQ = N_HEADS * PAGES_PER_HEAD


def _sattn_kernel(n_blk, pt_ref, top_ref, tab_ref, q_ref, kn_ref, vn_ref, b_last_ref, ck_hbm, cv_hbm,
                  o_ref, kbuf, vbuf, ksem, vsem):
    b = pl.program_id(0)
    slot = lax.rem(b, 2)

    def page_copies(seq, sl):
        out = []
        for h in range(N_HEADS):
            for t in range(MOBA_TOPK):
                blk = top_ref[seq, h * MOBA_TOPK + t]
                for r in range(PAGES_PER_BLOCK):
                    page = pt_ref[seq, PAGES_PER_BLOCK * blk + r]
                    idx = h * PAGES_PER_HEAD + t * PAGES_PER_BLOCK + r
                    out.append(pltpu.make_async_copy(ck_hbm.at[page, :, h, :], kbuf.at[sl, idx], ksem.at[sl]))
                    out.append(pltpu.make_async_copy(cv_hbm.at[page, :, h, :], vbuf.at[sl, idx], vsem.at[sl]))
        return out

    @pl.when(b == 0)
    def _():
        for cp in page_copies(0, 0):
            cp.start()

    @pl.when(b + 1 < pl.num_programs(0))
    def _():
        for cp in page_copies(b + 1, 1 - slot):
            cp.start()

    for cp in page_copies(b, slot):
        cp.wait()

    q = q_ref[...]
    vn = vn_ref[...]
    s_self = jnp.sum(q * kn_ref[...], axis=1, keepdims=True) * SCALE
    sub = lax.broadcasted_iota(I32, (N_HEADS, HEAD_DIM), 0)
    out = jnp.zeros((N_HEADS, HEAD_DIM), F32)
    n_keys = PAGES_PER_HEAD * PAGE_SIZE
    for h in range(N_HEADS):
        pages = slice(h * PAGES_PER_HEAD, (h + 1) * PAGES_PER_HEAD)
        kh = kbuf[slot, pages].reshape(n_keys, HEAD_DIM).astype(BF16)
        vh = vbuf[slot, pages].reshape(n_keys, HEAD_DIM).astype(BF16)
        q8 = jnp.broadcast_to(q[h:h + 1, :], (SUBLANES, HEAD_DIM)).astype(BF16)
        s = lax.dot_general(q8, kh, _NT, preferred_element_type=F32)[:1, :] * SCALE
        far_bias = tab_ref[REL_BUCKETS - 1, h]
        bias = []
        for t in range(MOBA_TOPK):
            is_last = top_ref[b, h * MOBA_TOPK + t] == n_blk - 1
            bias.append(jnp.where(is_last, b_last_ref[h:h + 1, :], far_bias))
        s = s + jnp.concatenate(bias, axis=1)
        sh = s_self[h:h + 1, :] + tab_ref[0, h]
        m = jnp.maximum(jnp.max(s, axis=1, keepdims=True), sh)
        p = jnp.exp(s - m)
        p_self = jnp.exp(sh - m)
        l = jnp.sum(p, axis=1, keepdims=True) + p_self
        p8 = jnp.broadcast_to(p, (SUBLANES, n_keys)).astype(BF16)
        o = jnp.dot(p8, vh, preferred_element_type=F32)[:1, :] + p_self * vn[h:h + 1, :]
        out = jnp.where(sub == h, o / l, out)
    o_ref[...] = out


def _sample_attention(q3, kn3, vn3, cache_k, cache_v, page_table, top, rel_table, bias_last):
    bs = q3.shape[0]
    n_blk = page_table.shape[1] * PAGE_SIZE // MOBA_BLOCK
    per_b = pl.BlockSpec((None, N_HEADS, HEAD_DIM), lambda b, pt, tp: (b, 0, 0))
    buf = pltpu.VMEM((2, PAGES_PER_SEQ, PAGE_SIZE, HEAD_DIM), F32)
    grid_spec = pltpu.PrefetchScalarGridSpec(
        num_scalar_prefetch=2, grid=(bs,),
        in_specs=[pl.BlockSpec(memory_space=pltpu.SMEM), per_b, per_b, per_b,
                  pl.BlockSpec((N_HEADS, MOBA_BLOCK), lambda b, pt, tp: (0, 0)),
                  pl.BlockSpec(memory_space=pl.ANY), pl.BlockSpec(memory_space=pl.ANY)],
        out_specs=per_b,
        scratch_shapes=[buf, buf, pltpu.SemaphoreType.DMA((2,)), pltpu.SemaphoreType.DMA((2,))])
    out = pl.pallas_call(
        functools.partial(_sattn_kernel, n_blk),
        out_shape=jax.ShapeDtypeStruct((bs, N_HEADS, HEAD_DIM), F32),
        grid_spec=grid_spec, compiler_params=_cparams(1), name="moba_sample")(
            page_table, top, rel_table, q3, kn3, vn3, bias_last, cache_k, cache_v)
    return out.reshape(bs, ATTN_WIDTH)


CONV_HALO = 32
CONV_ROWS = 32


def _ln_swish(y, g, b):
    mu = jnp.mean(y, axis=-1, keepdims=True)
    var = jnp.mean(jnp.square(y - mu), axis=-1, keepdims=True)
    yn = (y - mu) * lax.rsqrt(var + EPS) * g + b
    return yn * jax.nn.sigmoid(yn)


def _conv_kernel(tt, cur_ref, halo_ref, w_ref, b_ref, lg_ref, lb_ref, y_ref, ext_s, yc_s):
    t = pl.program_id(1)
    ch = cur_ref.shape[1]
    ext_s[CONV_HALO:, :] = cur_ref[...]

    @pl.when(t == 0)
    def _():
        ext_s[:CONV_HALO, :] = jnp.zeros((CONV_HALO, ch), F32)

    @pl.when(t > 0)
    def _():
        ext_s[:CONV_HALO, :] = halo_ref[...]

    lead = CONV_HALO - (CONV_WIDTH - 1)
    rc = CONV_ROWS

    def chunk(r, carry):
        r0 = pl.multiple_of(r * rc, rc)
        for lt in range(ch // LANES):
            ls = slice(lt * LANES, (lt + 1) * LANES)
            win = ext_s[pl.ds(r0, rc + CONV_HALO), ls]
            acc = jnp.zeros((rc, LANES), F32)
            for sh in range(SUBLANES):
                ws = win if sh == 0 else win[sh:sh + rc + CONV_HALO - SUBLANES]
                for a in range(CONV_HALO // SUBLANES + 1):
                    k = SUBLANES * a + sh - lead
                    if 0 <= k < CONV_WIDTH and SUBLANES * a + rc <= ws.shape[0]:
                        acc = acc + w_ref[k:k + 1, ls] * ws[SUBLANES * a:SUBLANES * a + rc]
            yc_s[pl.ds(r0, rc), ls] = acc + b_ref[:, ls]
        return carry

    lax.fori_loop(0, tt // rc, chunk, 0)
    y_ref[...] = _ln_swish(yc_s[...], lg_ref[...], lb_ref[...])


def _conv_prompt(u, w_dw, b_dw, ln_g, ln_b, batch, seq):
    ch = u.shape[1]
    tt = min(seq, 512)
    nt = seq // tt
    hpt = tt // CONV_HALO
    vec = lambda x: x.reshape(1, ch)
    return pl.pallas_call(
        functools.partial(_conv_kernel, tt),
        out_shape=jax.ShapeDtypeStruct((batch * seq, ch), F32), grid=(batch, nt),
        in_specs=[pl.BlockSpec((tt, ch), lambda b, t: (b * nt + t, 0)),
                  pl.BlockSpec((CONV_HALO, ch), lambda b, t: (jnp.maximum((b * nt + t) * hpt - 1, 0), 0)),
                  _resident(w_dw.shape), _resident((1, ch)), _resident((1, ch)), _resident((1, ch))],
        out_specs=pl.BlockSpec((tt, ch), lambda b, t: (b * nt + t, 0)),
        scratch_shapes=[pltpu.VMEM((tt + CONV_HALO, ch), F32), pltpu.VMEM((tt, ch), F32)],
        compiler_params=_cparams(2), name="conv_prompt")(u, u, w_dw, vec(b_dw), vec(ln_g), vec(ln_b))


def _conv_step_kernel(st_ref, u_ref, w_ref, b_ref, lg_ref, lb_ref, y_ref, ns_ref):
    hist = CONV_WIDTH - 1
    u = u_ref[...]
    acc = w_ref[hist:hist + 1, :] * u
    for k in range(hist):
        acc = acc + w_ref[k:k + 1, :] * st_ref[:, k, :]
    y_ref[...] = _ln_swish(acc + b_ref[...], lg_ref[...], lb_ref[...])
    for k in range(hist - 1):
        ns_ref[:, k, :] = st_ref[:, k + 1, :]
    ns_ref[:, hist - 1, :] = u


def _conv_step(state, u, w_dw, b_dw, ln_g, ln_b):
    bs, hist, ch = state.shape
    vec = lambda x: x.reshape(1, ch)
    return pl.pallas_call(
        _conv_step_kernel,
        out_shape=(jax.ShapeDtypeStruct((bs, ch), F32), jax.ShapeDtypeStruct((bs, hist, ch), F32)),
        compiler_params=pltpu.CompilerParams(vmem_limit_bytes=VMEM_LIMIT_BYTES),
        name="conv_step")(state, u, w_dw, vec(b_dw), vec(ln_g), vec(ln_b))


def _mix_kernel(attn_ref, conv_ref, x_ref, wo_ref, g_ref, wq_ref, h_ref, xn_ref, pq_ref):
    mix = jnp.concatenate([attn_ref[...], conv_ref[...]], axis=1).astype(BF16)
    h = x_ref[...] + jnp.dot(mix, wo_ref[...], preferred_element_type=F32)
    h_ref[...] = h
    xn = _rmsnorm(h, g_ref[...]).astype(BF16)
    xn_ref[...] = xn
    pq_ref[...] = jnp.dot(xn, wq_ref[...], preferred_element_type=F32)


def _mix_out(attn, conv, x, wo_bf16, g, wq_bf16):
    m, d = x.shape
    tm = min(m, 256)
    row = lambda i: (i, 0)
    half = attn.shape[1]
    return pl.pallas_call(
        _mix_kernel,
        out_shape=(jax.ShapeDtypeStruct((m, d), F32), jax.ShapeDtypeStruct((m, d), BF16),
                   jax.ShapeDtypeStruct((m, wq_bf16.shape[1]), F32)),
        grid=(m // tm,),
        in_specs=[pl.BlockSpec((tm, half), row), pl.BlockSpec((tm, half), row), pl.BlockSpec((tm, d), row),
                  _resident(wo_bf16.shape), _resident((1, d)), _resident(wq_bf16.shape)],
        out_specs=(pl.BlockSpec((tm, d), row), pl.BlockSpec((tm, d), row),
                   pl.BlockSpec((tm, wq_bf16.shape[1]), row)),
        compiler_params=_cparams(1), name="mix_out")(attn, conv, x, wo_bf16, g.reshape(1, d), wq_bf16)


ROUTE_TOKENS = LANES
_BIG = 1 << 20


def _candidate_layout():
    k = PEER_TOPK
    groups, flat = [], []
    for a in range(k // 2):
        nb = k // (a + 1)
        rows = -(-nb // SUBLANES) * SUBLANES
        groups.append((a, 1, rows))
        flat += [a * k + b if b < nb else _BIG for b in range(rows)]
    groups.append((k // 2, k // 2, 1))
    flat += [a * k for a in range(k // 2, k)]
    return groups, np.asarray(flat, np.int32)


def _route_kernel(pq_ref, sk_ref, flat_ref, e_ref, g_ref, et_s, gt_s):
    k = PEER_TOPK
    nk = PEER_NKEYS
    tn = ROUTE_TOKENS
    row = lax.broadcasted_iota(I32, (nk, tn), 0)
    groups, _ = _candidate_layout()
    flat = flat_ref[...]

    def half_topk(h, c):
        off = pl.multiple_of((h * 2 + c) * nk, nk)
        s = lax.dot_general(sk_ref[h, c], pq_ref[:, pl.ds(off, nk)], _NT,
                            precision=lax.Precision.HIGHEST, preferred_element_type=F32)
        ts, ti = [], []
        for _ in range(k):
            m = jnp.max(s, axis=0, keepdims=True)
            idx = jnp.min(jnp.where(s == m, row, nk), axis=0, keepdims=True)
            ts.append(m)
            ti.append(idx)
            s = jnp.where(row == idx, -jnp.inf, s)
        return jnp.concatenate(ts, axis=0), jnp.concatenate(ti, axis=0)

    def head(h, carry):
        s0, i0 = half_topk(h, 0)
        s1, i1 = half_topk(h, 1)
        cs, ce = [], []
        for a0, na, nb in groups:
            if na == 1:
                cs.append(s0[a0:a0 + 1] + s1[:nb])
                ce.append(i0[a0:a0 + 1] * nk + i1[:nb])
            else:
                cs.append(s0[a0:a0 + na] + s1[:1])
                ce.append(i0[a0:a0 + na] * nk + i1[:1])
        cand = jnp.where(flat < _BIG, jnp.concatenate(cs, axis=0), -jnp.inf)
        cexp = jnp.concatenate(ce, axis=0)
        bs, be = [], []
        for _ in range(k):
            m = jnp.max(cand, axis=0, keepdims=True)
            idx = jnp.min(jnp.where(cand == m, flat, _BIG), axis=0, keepdims=True)
            hit = flat == idx
            bs.append(m)
            be.append(jnp.max(jnp.where(hit, cexp, -1), axis=0, keepdims=True))
            cand = jnp.where(hit, -jnp.inf, cand)
        best = jnp.concatenate(bs, axis=0)
        p = jnp.exp(best - best[:1])
        r = pl.multiple_of(h * k, k)
        gt_s[pl.ds(r, k), :] = p / jnp.sum(p, axis=0, keepdims=True)
        et_s[pl.ds(r, k), :] = jnp.concatenate(be, axis=0)
        return carry

    lax.fori_loop(0, PEER_HEADS, head, 0, unroll=2)
    e_ref[...] = et_s[...].T
    g_ref[...] = gt_s[...].T


def _route(pq, sub_keys):
    n, qd = pq.shape
    tn = ROUTE_TOKENS
    _, flat = _candidate_layout()
    flat = jnp.asarray(np.broadcast_to(flat[:, None], (flat.shape[0], tn)))
    row = lambda i: (i, 0)
    return pl.pallas_call(
        _route_kernel,
        out_shape=(jax.ShapeDtypeStruct((n, PEER_SLOTS), I32), jax.ShapeDtypeStruct((n, PEER_SLOTS), F32)),
        grid=(n // tn,),
        in_specs=[pl.BlockSpec((tn, qd), row), _resident(sub_keys.shape), _resident(flat.shape)],
        out_specs=(pl.BlockSpec((tn, PEER_SLOTS), row), pl.BlockSpec((tn, PEER_SLOTS), row)),
        scratch_shapes=[pltpu.VMEM((PEER_SLOTS, tn), I32), pltpu.VMEM((PEER_SLOTS, tn), F32)],
        compiler_params=_cparams(1), name="peer_route")(pq, sub_keys, flat)


EXPERT_CHUNK = 1024
_NKEYS_BITS = PEER_NKEYS.bit_length() - 1
assert 1 << _NKEYS_BITS == PEER_NKEYS


def _split_expert(e):
    return lax.shift_right_logical(e, _NKEYS_BITS), e & (PEER_NKEYS - 1)


def _peer_up_kernel(xn_ref, e_ref, u_ref, hs_ref):
    c = pl.program_id(1)
    groups = EXPERT_CHUNK // PEER_NKEYS

    @pl.when(c == 0)
    def _():
        hs_ref[...] = jnp.zeros(hs_ref.shape, F32)

    hd = lax.dot_general(xn_ref[...], u_ref[...].astype(BF16), _NT,
                         preferred_element_type=F32)
    e = e_ref[...]
    i1, i2 = _split_expert(e)
    acc = hs_ref[...]
    for g in range(groups):
        got = jnp.take_along_axis(hd[:, g * PEER_NKEYS:(g + 1) * PEER_NKEYS], i2, axis=1)
        acc = jnp.where(i1 == c * groups + g, got, acc)
    hs_ref[...] = acc


def _peer_up(xn_bf16, e, u_tab):
    n, d = xn_bf16.shape
    n_exp = u_tab.shape[0]
    tn = 1024 if n % 1024 == 0 else (256 if n % 256 == 0 else n)
    return pl.pallas_call(
        _peer_up_kernel, out_shape=jax.ShapeDtypeStruct((n, PEER_SLOTS), F32),
        grid=(n // tn, n_exp // EXPERT_CHUNK),
        in_specs=[pl.BlockSpec((tn, d), lambda i, c: (i, 0)),
                  pl.BlockSpec((tn, PEER_SLOTS), lambda i, c: (i, 0)),
                  pl.BlockSpec((EXPERT_CHUNK, d), lambda i, c: (c, 0))],
        out_specs=pl.BlockSpec((tn, PEER_SLOTS), lambda i, c: (i, 0)),
        compiler_params=_cparams(2), name="peer_up")(xn_bf16, e, u_tab)


TOKEN_UNROLL = 8


DOWN_EXPERT_CHUNK = 512


def _peer_down_kernel(tn, e_ref, gate_ref, hs_ref, v_ref, h_ref, gf_ref, y_ref, act_s, a_s):
    c = pl.program_id(1)
    nk = PEER_NKEYS
    groups = DOWN_EXPERT_CHUNK // nk

    @pl.when(c == 0)
    def _():
        y_ref[...] = jnp.zeros(y_ref.shape, F32)
        hv = hs_ref[...]
        gelu = 0.5 * hv * (1.0 + lax.erf(hv * math.sqrt(0.5)))
        act_s[...] = gate_ref[...] * gelu
        sub = lax.broadcasted_iota(I32, (nk, PEER_SLOTS), 0)

        def token(n, carry):
            e = e_ref[pl.ds(n, 1), :]
            a = act_s[pl.ds(n, 1), :]
            i1, i2 = _split_expert(e)
            pt = jnp.where(sub == i1, a, 0.0).astype(BF16)
            qt = jnp.where(sub == i2, 1.0, 0.0).astype(BF16)
            a_s[pl.ds(pl.multiple_of(n * nk, nk), nk), :] = lax.dot_general(
                pt, qt, _NT, preferred_element_type=F32)
            return carry

        lax.fori_loop(0, tn, token, 0, unroll=TOKEN_UNROLL)

    lhs = jnp.concatenate(
        [a_s[pl.ds(c * groups + g, tn, stride=nk), :].astype(BF16) for g in range(groups)], axis=1)
    y_ref[...] += jnp.dot(lhs, v_ref[...], preferred_element_type=F32)

    @pl.when(c == pl.num_programs(1) - 1)
    def _():
        y_ref[...] = _rmsnorm(h_ref[...] + y_ref[...], gf_ref[...])


def _peer_down(e, gate, hs, v_bf16, h, g_final):
    n, d = h.shape
    n_exp = v_bf16.shape[0]
    tn = 512 if n % 512 == 0 else n
    tok = lambda i, c: (i, 0)
    return pl.pallas_call(
        functools.partial(_peer_down_kernel, tn),
        out_shape=jax.ShapeDtypeStruct((n, d), F32), grid=(n // tn, n_exp // DOWN_EXPERT_CHUNK),
        in_specs=[pl.BlockSpec((tn, PEER_SLOTS), tok), pl.BlockSpec((tn, PEER_SLOTS), tok),
                  pl.BlockSpec((tn, PEER_SLOTS), tok),
                  pl.BlockSpec((DOWN_EXPERT_CHUNK, d), lambda i, c: (c, 0)),
                  pl.BlockSpec((tn, d), tok, pipeline_mode=pl.Buffered(1)), _resident((1, d))],
        out_specs=pl.BlockSpec((tn, d), tok),
        scratch_shapes=[pltpu.VMEM((tn, PEER_SLOTS), F32), pltpu.VMEM((tn * PEER_NKEYS, PEER_NKEYS), F32)],
        compiler_params=_cparams(2, DOWN_VMEM_LIMIT_BYTES), name="peer_down")(
            e, gate, hs, v_bf16, h, g_final.reshape(1, d))


def _peer_and_final_norm(h, xn_bf16, pq, sub_keys, u_tab, v_bf16, g_final):
    n = h.shape[0]
    n_pad = -(-n // ROUTE_TOKENS) * ROUTE_TOKENS
    pq_pad = pq if n_pad == n else jnp.pad(pq, ((0, n_pad - n), (0, 0)))
    e, gate = _route(pq_pad, sub_keys)
    e, gate = e[:n], gate[:n]
    hs = _peer_up(xn_bf16, e, u_tab)
    return _peer_down(e, gate, hs, v_bf16, h, g_final)


def kernel(x_prompt, x_sample, cache_k, cache_v, state_conv, page_table, rel_bias_table, norm_mix_g, w_in,
           w_dw, b_dw, conv_ln_g, conv_ln_b, w_out, norm_ffn_g, peer_wq, peer_sub_keys, peer_u, peer_v,
           final_norm_g):
    batch, seq, d = x_prompt.shape
    bs, s_new, _ = x_sample.shape
    depth = w_in.shape[0]
    assert s_new == 1 and depth == 1 and seq % MOBA_BLOCK == 0 and seq // MOBA_BLOCK >= MOBA_TOPK
    hist = CONV_WIDTH - 1
    blk = MOBA_BLOCK

    dist = np.arange(blk)[None, :] - np.arange(blk)[:, None]
    bkt_prompt = jnp.concatenate([_rel_bucket(dist), _rel_bucket(dist + blk)], axis=0)
    bias_prompt = _bias_lookup(rel_bias_table, bkt_prompt).reshape(N_HEADS, 2, blk, blk)
    bkt_last = jnp.broadcast_to(_rel_bucket(blk - np.arange(blk))[None, :], (SUBLANES, blk))
    bias_last = _bias_lookup(rel_bias_table, bkt_last)[:, 0, :]

    hp = x_prompt.reshape(batch * seq, d)
    hs_ = x_sample.reshape(bs, d)
    l = 0
    w_in_b = w_in[l].astype(BF16)
    w_out_b = w_out[l].astype(BF16)
    wq_b = peer_wq[l].astype(BF16)
    u_b = peer_u[l]
    v_b = peer_v[l].astype(BF16)

    qp, kp, vp, up = _proj(hp, norm_mix_g[l], w_in_b)
    attn_p = _prompt_attention(qp, kp, vp, bias_prompt, rel_bias_table, batch, seq)
    conv_p = _conv_prompt(up, w_dw[l], b_dw[l], conv_ln_g[l], conv_ln_b[l], batch, seq)
    h_p, xn_p, pq_p = _mix_out(attn_p, conv_p, hp, w_out_b, norm_ffn_g[l], wq_b)
    y_p = _peer_and_final_norm(h_p, xn_p, pq_p, peer_sub_keys[l], u_b, v_b, final_norm_g)

    qs, ks, vs, us = _proj(hs_, norm_mix_g[l], w_in_b)
    q3 = qs.reshape(bs, N_HEADS, HEAD_DIM)
    top = _sample_gate(q3, cache_k[l], page_table)
    attn_s = _sample_attention(q3, ks.reshape(q3.shape), vs.reshape(q3.shape), cache_k[l], cache_v[l],
                               page_table, top, rel_bias_table, bias_last)
    conv_s, state_new = _conv_step(state_conv[l], us, w_dw[l], b_dw[l], conv_ln_g[l], conv_ln_b[l])
    h_s, xn_s, pq_s = _mix_out(attn_s, conv_s, hs_, w_out_b, norm_ffn_g[l], wq_b)
    y_s = _peer_and_final_norm(h_s, xn_s, pq_s, peer_sub_keys[l], u_b, v_b, final_norm_g)

    kv_p = (1, batch, seq, N_HEADS, HEAD_DIM)
    kv_s = (1, bs, 1, N_HEADS, HEAD_DIM)
    conv_prompt_new = up.reshape(batch, seq, -1)[:, seq - hist:, :][None]
    return (y_p.reshape(batch, seq, d), y_s.reshape(bs, 1, d),
            kp.reshape(kv_p), vp.reshape(kv_p), conv_prompt_new,
            ks.reshape(kv_s), vs.reshape(kv_s), state_new[None])
```

```python
import functools
import math

import numpy as np
import jax
import jax.numpy as jnp
from jax import lax
from jax.experimental import pallas as pl
from jax.experimental.pallas import tpu as pltpu

F32, BF16, I32 = jnp.float32, jnp.bfloat16, jnp.int32

N_HEADS = 8
HEAD_DIM = 128
ATTN_WIDTH = N_HEADS * HEAD_DIM
MOBA_BLOCK = 256
MOBA_TOPK = 3
PAGE_SIZE = 128
REL_BUCKETS = 32
REL_MAX_DIST = 128
CONV_WIDTH = 31
PEER_HEADS = 8
PEER_NKEYS = 128
PEER_TOPK = 16
PEER_SLOTS = PEER_HEADS * PEER_TOPK
EPS = 1e-6
NEG = -1e30
SCALE = HEAD_DIM ** -0.5

LANES = 128
SUBLANES = 8
MXU_WIDTH = 256
VMEM_LIMIT_BYTES = 56 * 1024 * 1024

_NT = (((1,), (1,)), ((), ()))


def _cparams(n_axes, vmem_limit_bytes=VMEM_LIMIT_BYTES):
    return pltpu.CompilerParams(dimension_semantics=("arbitrary",) * n_axes,
                                vmem_limit_bytes=vmem_limit_bytes)


def _resident(shape):
    nd = len(shape)
    return pl.BlockSpec(shape, lambda *_: (0,) * nd, pipeline_mode=pl.Buffered(1))


def _rmsnorm(x, g):
    return x * lax.rsqrt(jnp.mean(x * x, axis=-1, keepdims=True) + EPS) * g


def _proj_kernel(x_ref, g_ref, w_ref, q_ref, k_ref, v_ref, u_ref):
    xn = _rmsnorm(x_ref[...], g_ref[...]).astype(BF16)
    c = ATTN_WIDTH

    def mm(j):
        return jnp.dot(xn, w_ref[:, j * c:(j + 1) * c], preferred_element_type=F32)

    q_ref[...] = mm(0)
    k_ref[...] = mm(1)
    v_ref[...] = mm(2)
    u_ref[...] = mm(3) * jax.nn.sigmoid(mm(4))


def _proj(x, g, w_bf16):
    m, d = x.shape
    tm = min(m, 256)
    row = lambda i: (i, 0)
    out = jax.ShapeDtypeStruct((m, ATTN_WIDTH), F32)
    return pl.pallas_call(
        _proj_kernel, out_shape=(out,) * 4, grid=(m // tm,),
        in_specs=[pl.BlockSpec((tm, d), row), _resident((1, d)), _resident(w_bf16.shape)],
        out_specs=(pl.BlockSpec((tm, ATTN_WIDTH), row),) * 4,
        compiler_params=_cparams(1), name="proj")(x, g.reshape(1, d), w_bf16)


MASKED_BUCKET = REL_BUCKETS


def _rel_bucket(dist):
    dist = jnp.asarray(dist, I32)
    n = jnp.maximum(dist, 0)
    max_exact = REL_BUCKETS // 2
    nf = jnp.maximum(n, 1).astype(F32)
    large = max_exact + (jnp.log(nf / max_exact) / math.log(REL_MAX_DIST / max_exact)
                         * (REL_BUCKETS - max_exact)).astype(I32)
    bucket = jnp.where(n < max_exact, n, jnp.minimum(large, REL_BUCKETS - 1))
    return jnp.where(dist < 0, MASKED_BUCKET, bucket)


def _bias_kernel(tab_ref, bkt_ref, o_ref):
    h = pl.program_id(0)
    bkt = bkt_ref[...]
    acc = jnp.full(bkt.shape, NEG, F32)
    for b in range(REL_BUCKETS):
        acc = jnp.where(bkt == b, tab_ref[b, h], acc)
    o_ref[...] = acc


def _bias_lookup(rel_table, bkt):
    r, c = bkt.shape
    return pl.pallas_call(
        _bias_kernel, out_shape=jax.ShapeDtypeStruct((N_HEADS, r, c), F32), grid=(N_HEADS,),
        in_specs=[pl.BlockSpec(memory_space=pltpu.SMEM), _resident((r, c))],
        out_specs=pl.BlockSpec((None, r, c), lambda h: (h, 0, 0)),
        compiler_params=_cparams(1), name="rel_bias")(rel_table, bkt)


ATTN_WIDE_GROUP = 4
PAGES_PER_BLOCK = MOBA_BLOCK // PAGE_SIZE
PAGES_PER_STEP = 16


def _gate_step(s, n_blk, q_ref, pages, g_ref, top_ref):
    blocks_per_step = PAGES_PER_STEP // PAGES_PER_BLOCK
    q = q_ref[...]
    lane = lax.broadcasted_iota(I32, (N_HEADS, n_blk), 1)
    acc = g_ref[...]
    for p in range(blocks_per_step):
        bsum = functools.reduce(jnp.add, [jnp.sum(pages[PAGES_PER_BLOCK * p + r][...], axis=0)
                                          for r in range(PAGES_PER_BLOCK)])
        gn = jnp.sum(q * (bsum * (1.0 / MOBA_BLOCK)), axis=1, keepdims=True)
        acc = jnp.where(lane == s * blocks_per_step + p, gn, acc)
    g_ref[...] = acc

    g = acc
    out_lane = lax.broadcasted_iota(I32, (N_HEADS, LANES), 1)
    top = jnp.zeros((N_HEADS, LANES), I32)
    for t in range(MOBA_TOPK):
        m = jnp.max(g, axis=1, keepdims=True)
        idx = jnp.min(jnp.where(g == m, lane, n_blk), axis=1, keepdims=True)
        top = jnp.where(out_lane == t, idx, top)
        g = jnp.where(lane == idx, -jnp.inf, g)
    top_ref[...] = top


def _attn_kernel(nb, steps_per_seq, n_blk, pt_ref, tab_ref, q_ref, k_ref, v_ref, bias_ref, qs_ref, *rest):
    pages = rest[:PAGES_PER_STEP]
    o_ref, g_ref, top_ref, kb_s, vt_s, km_s, far_s, m_s, l_s, acc_s = rest[PAGES_PER_STEP:]
    h = pl.program_id(1)
    i = pl.program_id(2)
    blk = MOBA_BLOCK
    gate_step = lax.rem((pl.program_id(0) * N_HEADS + h) * nb + i, steps_per_seq)

    @pl.when(gate_step == 0)
    def _():
        g_ref[...] = jnp.zeros(g_ref.shape, F32)

    @pl.when(i == 0)
    def _():
        kb_s[...] = k_ref[...].astype(BF16)
        for j in range(nb):
            rows = slice(j * blk, (j + 1) * blk)
            km_s[j:j + 1, :] = jnp.mean(k_ref[rows, :], axis=0, keepdims=True)
            vt_s[:, rows] = v_ref[rows, :].T.astype(BF16)

    _gate_step(gate_step, n_blk, qs_ref, pages, g_ref, top_ref)

    q = q_ref[...]
    qt = q.T.astype(BF16)

    gate = lax.dot_general(km_s[...], q, _NT, precision=lax.Precision.HIGHEST,
                           preferred_element_type=F32)
    sub = lax.broadcasted_iota(I32, (nb, blk), 0)
    gate = jnp.where(sub < i, gate, NEG)
    rank = jnp.zeros((nb, blk), I32)
    for jp in range(nb):
        gj = gate[jp:jp + 1, :]
        rank = rank + jnp.where(gj == gate, jnp.where(jp < sub, 1, 0),
                                jnp.where(gj > gate, 1, 0))
    selneg = jnp.where((rank < MOBA_TOPK) & (sub < i), 0.0, NEG)
    far = selneg + tab_ref[REL_BUCKETS - 1, h]

    def row_of(x, j):
        return jnp.where(j >= 0, jnp.sum(jnp.where(sub == j, x, 0.0), axis=0, keepdims=True), NEG)

    n_first = ATTN_WIDE_GROUP
    n_rest = jnp.maximum(i + 1 - n_first, 0)
    far_s[...] = jnp.where(sub < n_rest, far, NEG)

    def partial_softmax(blocks):
        rows = [pl.multiple_of(j * blk, blk) for j, _ in blocks]
        sts = [jnp.dot(kb_s[pl.ds(r, blk), :], qt, preferred_element_type=F32) * SCALE + extra
               for r, (_, extra) in zip(rows, blocks)]
        ms = [jnp.max(st, axis=0, keepdims=True) for st in sts]
        ps = [jnp.exp(st - m) for st, m in zip(sts, ms)]
        ls = [jnp.sum(p, axis=0, keepdims=True) for p in ps]
        os_ = [jnp.dot(vt_s[:, pl.ds(r, blk)], p.astype(BF16), preferred_element_type=F32)
               for r, p in zip(rows, ps)]
        return list(zip(ms, ls, os_))

    def merge(parts):
        m = functools.reduce(jnp.maximum, [pm for pm, _, _ in parts])
        w = [jnp.exp(pm - m) for pm, _, _ in parts]
        l = functools.reduce(jnp.add, [wi * pl_ for wi, (_, pl_, _) in zip(w, parts)])
        o = functools.reduce(jnp.add, [wi * po for wi, (_, _, po) in zip(w, parts)])
        return m, l, o

    first = [(i, bias_ref[0]), (jnp.maximum(i - 1, 0), bias_ref[1] + row_of(selneg, i - 1))]
    first += [(jnp.maximum(i - back, 0), row_of(far, i - back)) for back in range(2, n_first)]
    m0, l0, o0 = merge(partial_softmax(first))
    m_s[...] = m0
    l_s[...] = l0
    acc_s[...] = o0

    def group(g, carry):
        j0 = g * ATTN_WIDE_GROUP
        parts = partial_softmax([(j0 + u, far_s[pl.ds(j0 + u, 1), :]) for u in range(ATTN_WIDE_GROUP)])
        m, l, o = merge([(m_s[...], l_s[...], acc_s[...])] + parts)
        m_s[...] = m
        l_s[...] = l
        acc_s[...] = o
        return carry

    lax.fori_loop(0, (n_rest + ATTN_WIDE_GROUP - 1) // ATTN_WIDE_GROUP, group, 0)

    o_ref[...] = (acc_s[...] / l_s[...]).T


def _prompt_attention_and_sample_gate(q, k, v, bias, rel_table, batch, seq, q_sample, cache_k, page_table):
    nb = seq // MOBA_BLOCK
    blk = MOBA_BLOCK
    bs, n_pages = page_table.shape
    n_blk = n_pages // PAGES_PER_BLOCK
    steps_per_seq = n_pages // PAGES_PER_STEP
    assert n_pages % PAGES_PER_STEP == 0 and n_blk >= MOBA_TOPK
    assert batch * N_HEADS * nb == bs * steps_per_seq

    def flat(b, h, i):
        return (b * N_HEADS + h) * nb + i

    qo = lambda b, h, i, pt: (b * nb + i, h)
    kv = lambda b, h, i, pt: (b, h)
    per_seq = lambda b, h, i, pt: (flat(b, h, i) // steps_per_seq, 0, 0)

    def page_spec(r):
        def index(b, h, i, pt):
            f = flat(b, h, i)
            return pt[f // steps_per_seq, (f % steps_per_seq) * PAGES_PER_STEP + r], 0, 0, 0
        return pl.BlockSpec((None, PAGE_SIZE, N_HEADS, HEAD_DIM), index)

    grid_spec = pltpu.PrefetchScalarGridSpec(
        num_scalar_prefetch=1, grid=(batch, N_HEADS, nb),
        in_specs=[pl.BlockSpec(memory_space=pltpu.SMEM),
                  pl.BlockSpec((blk, HEAD_DIM), qo),
                  pl.BlockSpec((seq, HEAD_DIM), kv),
                  pl.BlockSpec((seq, HEAD_DIM), kv),
                  pl.BlockSpec((None, 2, blk, blk), lambda b, h, i, pt: (h, 0, 0, 0)),
                  pl.BlockSpec((None, N_HEADS, HEAD_DIM), per_seq)]
        + [page_spec(r) for r in range(PAGES_PER_STEP)],
        out_specs=(pl.BlockSpec((blk, HEAD_DIM), qo),
                   pl.BlockSpec((None, N_HEADS, n_blk), per_seq),
                   pl.BlockSpec((None, N_HEADS, LANES), per_seq)),
        scratch_shapes=[pltpu.VMEM((seq, HEAD_DIM), BF16), pltpu.VMEM((HEAD_DIM, seq), BF16),
                        pltpu.VMEM((nb, HEAD_DIM), F32), pltpu.VMEM((nb, blk), F32),
                        pltpu.VMEM((1, blk), F32), pltpu.VMEM((1, blk), F32),
                        pltpu.VMEM((HEAD_DIM, blk), F32)])
    attn, _, top = pl.pallas_call(
        functools.partial(_attn_kernel, nb, steps_per_seq, n_blk),
        out_shape=(jax.ShapeDtypeStruct((batch * seq, ATTN_WIDTH), F32),
                   jax.ShapeDtypeStruct((bs, N_HEADS, n_blk), F32),
                   jax.ShapeDtypeStruct((bs, N_HEADS, LANES), I32)),
        grid_spec=grid_spec, compiler_params=_cparams(3), name="moba_prompt")(
            page_table, rel_table, q, k, v, bias, q_sample, *([cache_k] * PAGES_PER_STEP))
    return attn, top[:, :, :MOBA_TOPK].reshape(bs, N_HEADS * MOBA_TOPK)


PAGES_PER_HEAD = MOBA_TOPK * PAGES_PER_BLOCK
PAGES_PER_SEQ = N_HEADS * PAGES_PER_HEAD


def _sattn_kernel(n_blk, pt_ref, top_ref, tab_ref, q_ref, kn_ref, vn_ref, b_last_ref, ck_hbm, cv_hbm,
                  o_ref, kbuf, vbuf, ksem, vsem):
    b = pl.program_id(0)
    slot = lax.rem(b, 2)

    def page_copies(seq, sl):
        out = []
        for h in range(N_HEADS):
            for t in range(MOBA_TOPK):
                blk = top_ref[seq, h * MOBA_TOPK + t]
                for r in range(PAGES_PER_BLOCK):
                    page = pt_ref[seq, PAGES_PER_BLOCK * blk + r]
                    idx = h * PAGES_PER_HEAD + t * PAGES_PER_BLOCK + r
                    out.append(pltpu.make_async_copy(ck_hbm.at[page, :, h, :], kbuf.at[sl, idx], ksem.at[sl]))
                    out.append(pltpu.make_async_copy(cv_hbm.at[page, :, h, :], vbuf.at[sl, idx], vsem.at[sl]))
        return out

    @pl.when(b == 0)
    def _():
        for cp in page_copies(0, 0):
            cp.start()

    @pl.when(b + 1 < pl.num_programs(0))
    def _():
        for cp in page_copies(b + 1, 1 - slot):
            cp.start()

    for cp in page_copies(b, slot):
        cp.wait()

    q = q_ref[...]
    vn = vn_ref[...]
    s_self = jnp.sum(q * kn_ref[...], axis=1, keepdims=True) * SCALE
    sub = lax.broadcasted_iota(I32, (N_HEADS, HEAD_DIM), 0)
    out = jnp.zeros((N_HEADS, HEAD_DIM), F32)
    n_keys = PAGES_PER_HEAD * PAGE_SIZE
    for h in range(N_HEADS):
        pages = slice(h * PAGES_PER_HEAD, (h + 1) * PAGES_PER_HEAD)
        kh = kbuf[slot, pages].reshape(n_keys, HEAD_DIM).astype(BF16)
        vh = vbuf[slot, pages].reshape(n_keys, HEAD_DIM).astype(BF16)
        q8 = jnp.broadcast_to(q[h:h + 1, :], (SUBLANES, HEAD_DIM)).astype(BF16)
        s = lax.dot_general(q8, kh, _NT, preferred_element_type=F32)[:1, :] * SCALE
        far_bias = tab_ref[REL_BUCKETS - 1, h]
        bias = []
        for t in range(MOBA_TOPK):
            is_last = top_ref[b, h * MOBA_TOPK + t] == n_blk - 1
            bias.append(jnp.where(is_last, b_last_ref[h:h + 1, :], far_bias))
        s = s + jnp.concatenate(bias, axis=1)
        sh = s_self[h:h + 1, :] + tab_ref[0, h]
        m = jnp.maximum(jnp.max(s, axis=1, keepdims=True), sh)
        p = jnp.exp(s - m)
        p_self = jnp.exp(sh - m)
        l = jnp.sum(p, axis=1, keepdims=True) + p_self
        p8 = jnp.broadcast_to(p, (SUBLANES, n_keys)).astype(BF16)
        o = jnp.dot(p8, vh, preferred_element_type=F32)[:1, :] + p_self * vn[h:h + 1, :]
        out = jnp.where(sub == h, o / l, out)
    o_ref[...] = out


def _sample_attention(q3, kn3, vn3, cache_k, cache_v, page_table, top, rel_table, bias_last):
    bs = q3.shape[0]
    n_blk = page_table.shape[1] * PAGE_SIZE // MOBA_BLOCK
    per_b = pl.BlockSpec((None, N_HEADS, HEAD_DIM), lambda b, pt, tp: (b, 0, 0))
    buf = pltpu.VMEM((2, PAGES_PER_SEQ, PAGE_SIZE, HEAD_DIM), F32)
    grid_spec = pltpu.PrefetchScalarGridSpec(
        num_scalar_prefetch=2, grid=(bs,),
        in_specs=[pl.BlockSpec(memory_space=pltpu.SMEM), per_b, per_b, per_b,
                  pl.BlockSpec((N_HEADS, MOBA_BLOCK), lambda b, pt, tp: (0, 0)),
                  pl.BlockSpec(memory_space=pl.ANY), pl.BlockSpec(memory_space=pl.ANY)],
        out_specs=per_b,
        scratch_shapes=[buf, buf, pltpu.SemaphoreType.DMA((2,)), pltpu.SemaphoreType.DMA((2,))])
    out = pl.pallas_call(
        functools.partial(_sattn_kernel, n_blk),
        out_shape=jax.ShapeDtypeStruct((bs, N_HEADS, HEAD_DIM), F32),
        grid_spec=grid_spec, compiler_params=_cparams(1), name="moba_sample")(
            page_table, top, rel_table, q3, kn3, vn3, bias_last, cache_k, cache_v)
    return out.reshape(bs, ATTN_WIDTH)


CONV_HALO = 32
CONV_ROWS = 32


def _ln_swish(y, g, b):
    mu = jnp.mean(y, axis=-1, keepdims=True)
    var = jnp.mean(jnp.square(y - mu), axis=-1, keepdims=True)
    yn = (y - mu) * lax.rsqrt(var + EPS) * g + b
    return yn * jax.nn.sigmoid(yn)


def _conv_kernel(tt, cur_ref, halo_ref, w_ref, b_ref, lg_ref, lb_ref, y_ref, ext_s, yc_s):
    t = pl.program_id(1)
    ch = cur_ref.shape[1]
    ext_s[CONV_HALO:, :] = cur_ref[...]

    @pl.when(t == 0)
    def _():
        ext_s[:CONV_HALO, :] = jnp.zeros((CONV_HALO, ch), F32)

    @pl.when(t > 0)
    def _():
        ext_s[:CONV_HALO, :] = halo_ref[...]

    lead = CONV_HALO - (CONV_WIDTH - 1)
    rc = CONV_ROWS

    def chunk(r, carry):
        r0 = pl.multiple_of(r * rc, rc)
        for lt in range(ch // LANES):
            ls = slice(lt * LANES, (lt + 1) * LANES)
            win = ext_s[pl.ds(r0, rc + CONV_HALO), ls]
            acc = jnp.zeros((rc, LANES), F32)
            for sh in range(SUBLANES):
                ws = win if sh == 0 else win[sh:sh + rc + CONV_HALO - SUBLANES]
                for a in range(CONV_HALO // SUBLANES + 1):
                    k = SUBLANES * a + sh - lead
                    if 0 <= k < CONV_WIDTH and SUBLANES * a + rc <= ws.shape[0]:
                        acc = acc + w_ref[k:k + 1, ls] * ws[SUBLANES * a:SUBLANES * a + rc]
            yc_s[pl.ds(r0, rc), ls] = acc + b_ref[:, ls]
        return carry

    lax.fori_loop(0, tt // rc, chunk, 0)
    y_ref[...] = _ln_swish(yc_s[...], lg_ref[...], lb_ref[...])


def _conv_prompt(u, w_dw, b_dw, ln_g, ln_b, batch, seq):
    ch = u.shape[1]
    tt = min(seq, 512)
    nt = seq // tt
    hpt = tt // CONV_HALO
    vec = lambda x: x.reshape(1, ch)
    return pl.pallas_call(
        functools.partial(_conv_kernel, tt),
        out_shape=jax.ShapeDtypeStruct((batch * seq, ch), F32), grid=(batch, nt),
        in_specs=[pl.BlockSpec((tt, ch), lambda b, t: (b * nt + t, 0)),
                  pl.BlockSpec((CONV_HALO, ch), lambda b, t: (jnp.maximum((b * nt + t) * hpt - 1, 0), 0)),
                  _resident(w_dw.shape), _resident((1, ch)), _resident((1, ch)), _resident((1, ch))],
        out_specs=pl.BlockSpec((tt, ch), lambda b, t: (b * nt + t, 0)),
        scratch_shapes=[pltpu.VMEM((tt + CONV_HALO, ch), F32), pltpu.VMEM((tt, ch), F32)],
        compiler_params=_cparams(2), name="conv_prompt")(u, u, w_dw, vec(b_dw), vec(ln_g), vec(ln_b))


def _conv_step_kernel(st_ref, u_ref, w_ref, b_ref, lg_ref, lb_ref, y_ref, ns_ref):
    hist = CONV_WIDTH - 1
    u = u_ref[...]
    acc = w_ref[hist:hist + 1, :] * u
    for k in range(hist):
        acc = acc + w_ref[k:k + 1, :] * st_ref[:, k, :]
    y_ref[...] = _ln_swish(acc + b_ref[...], lg_ref[...], lb_ref[...])
    for k in range(hist - 1):
        ns_ref[:, k, :] = st_ref[:, k + 1, :]
    ns_ref[:, hist - 1, :] = u


def _conv_step(state, u, w_dw, b_dw, ln_g, ln_b):
    bs, hist, ch = state.shape
    vec = lambda x: x.reshape(1, ch)
    return pl.pallas_call(
        _conv_step_kernel,
        out_shape=(jax.ShapeDtypeStruct((bs, ch), F32), jax.ShapeDtypeStruct((bs, hist, ch), F32)),
        compiler_params=pltpu.CompilerParams(vmem_limit_bytes=VMEM_LIMIT_BYTES),
        name="conv_step")(state, u, w_dw, vec(b_dw), vec(ln_g), vec(ln_b))


def _mix_kernel(attn_ref, conv_ref, x_ref, wo_ref, g_ref, wq_ref, h_ref, xn_ref, pq_ref):
    mix = jnp.concatenate([attn_ref[...], conv_ref[...]], axis=1).astype(BF16)
    h = x_ref[...] + jnp.dot(mix, wo_ref[...], preferred_element_type=F32)
    h_ref[...] = h
    xn = _rmsnorm(h, g_ref[...]).astype(BF16)
    xn_ref[...] = xn
    pq_ref[...] = jnp.dot(xn, wq_ref[...], preferred_element_type=F32)


def _mix_out(attn, conv, x, wo_bf16, g, wq_bf16):
    m, d = x.shape
    tm = min(m, 256)
    row = lambda i: (i, 0)
    half = attn.shape[1]
    return pl.pallas_call(
        _mix_kernel,
        out_shape=(jax.ShapeDtypeStruct((m, d), F32), jax.ShapeDtypeStruct((m, d), BF16),
                   jax.ShapeDtypeStruct((m, wq_bf16.shape[1]), F32)),
        grid=(m // tm,),
        in_specs=[pl.BlockSpec((tm, half), row), pl.BlockSpec((tm, half), row), pl.BlockSpec((tm, d), row),
                  _resident(wo_bf16.shape), _resident((1, d)), _resident(wq_bf16.shape)],
        out_specs=(pl.BlockSpec((tm, d), row), pl.BlockSpec((tm, d), row),
                   pl.BlockSpec((tm, wq_bf16.shape[1]), row)),
        compiler_params=_cparams(1), name="mix_out")(attn, conv, x, wo_bf16, g.reshape(1, d), wq_bf16)


ROUTE_TOKENS = LANES
_BIG = 1 << 20


def _candidate_layout():
    k = PEER_TOPK
    groups, flat = [], []
    for a in range(k // 2):
        nb = k // (a + 1)
        rows = -(-nb // SUBLANES) * SUBLANES
        groups.append((a, 1, rows))
        flat += [a * k + b if b < nb else _BIG for b in range(rows)]
    groups.append((k // 2, k // 2, 1))
    flat += [a * k for a in range(k // 2, k)]
    return groups, np.asarray(flat, np.int32)


def _route_kernel(pq_ref, sk_ref, flat_ref, e_ref, g_ref, et_s, gt_s):
    k = PEER_TOPK
    nk = PEER_NKEYS
    tn = ROUTE_TOKENS
    row = lax.broadcasted_iota(I32, (nk, tn), 0)
    groups, _ = _candidate_layout()
    flat = flat_ref[...]

    def half_topk(h, c):
        off = pl.multiple_of((h * 2 + c) * nk, nk)
        s = lax.dot_general(sk_ref[h, c], pq_ref[:, pl.ds(off, nk)], _NT,
                            precision=lax.Precision.HIGHEST, preferred_element_type=F32)
        ts, ti = [], []
        for _ in range(k):
            m = jnp.max(s, axis=0, keepdims=True)
            idx = jnp.min(jnp.where(s == m, row, nk), axis=0, keepdims=True)
            ts.append(m)
            ti.append(idx)
            s = jnp.where(row == idx, -jnp.inf, s)
        return jnp.concatenate(ts, axis=0), jnp.concatenate(ti, axis=0)

    def head(h, carry):
        s0, i0 = half_topk(h, 0)
        s1, i1 = half_topk(h, 1)
        cs, ce = [], []
        for a0, na, nb in groups:
            if na == 1:
                cs.append(s0[a0:a0 + 1] + s1[:nb])
                ce.append(i0[a0:a0 + 1] * nk + i1[:nb])
            else:
                cs.append(s0[a0:a0 + na] + s1[:1])
                ce.append(i0[a0:a0 + na] * nk + i1[:1])
        cand = jnp.where(flat < _BIG, jnp.concatenate(cs, axis=0), -jnp.inf)
        cexp = jnp.concatenate(ce, axis=0)
        bs, be = [], []
        for _ in range(k):
            m = jnp.max(cand, axis=0, keepdims=True)
            idx = jnp.min(jnp.where(cand == m, flat, _BIG), axis=0, keepdims=True)
            hit = flat == idx
            bs.append(m)
            be.append(jnp.max(jnp.where(hit, cexp, -1), axis=0, keepdims=True))
            cand = jnp.where(hit, -jnp.inf, cand)
        best = jnp.concatenate(bs, axis=0)
        p = jnp.exp(best - best[:1])
        r = pl.multiple_of(h * k, k)
        gt_s[pl.ds(r, k), :] = p / jnp.sum(p, axis=0, keepdims=True)
        et_s[pl.ds(r, k), :] = jnp.concatenate(be, axis=0)
        return carry

    lax.fori_loop(0, PEER_HEADS, head, 0, unroll=2)
    e_ref[...] = et_s[...].T
    g_ref[...] = gt_s[...].T


def _route(pq, sub_keys):
    n, qd = pq.shape
    tn = ROUTE_TOKENS
    _, flat = _candidate_layout()
    flat = jnp.asarray(np.broadcast_to(flat[:, None], (flat.shape[0], tn)))
    row = lambda i: (i, 0)
    return pl.pallas_call(
        _route_kernel,
        out_shape=(jax.ShapeDtypeStruct((n, PEER_SLOTS), I32), jax.ShapeDtypeStruct((n, PEER_SLOTS), F32)),
        grid=(n // tn,),
        in_specs=[pl.BlockSpec((tn, qd), row), _resident(sub_keys.shape), _resident(flat.shape)],
        out_specs=(pl.BlockSpec((tn, PEER_SLOTS), row), pl.BlockSpec((tn, PEER_SLOTS), row)),
        scratch_shapes=[pltpu.VMEM((PEER_SLOTS, tn), I32), pltpu.VMEM((PEER_SLOTS, tn), F32)],
        compiler_params=_cparams(1), name="peer_route")(pq, sub_keys, flat)


EXPERT_CHUNK = 1024
_NKEYS_BITS = PEER_NKEYS.bit_length() - 1
assert 1 << _NKEYS_BITS == PEER_NKEYS


def _split_expert(e):
    return lax.shift_right_logical(e, _NKEYS_BITS), e & (PEER_NKEYS - 1)


def _peer_up_kernel(xn_ref, e_ref, u_ref, hs_ref):
    c = pl.program_id(1)
    groups = EXPERT_CHUNK // PEER_NKEYS

    @pl.when(c == 0)
    def _():
        hs_ref[...] = jnp.zeros(hs_ref.shape, F32)

    xn = xn_ref[...]
    per_dot = MXU_WIDTH // PEER_NKEYS
    hd = [lax.dot_general(xn, u_ref[j * MXU_WIDTH:(j + 1) * MXU_WIDTH, :].astype(BF16), _NT,
                          preferred_element_type=F32) for j in range(EXPERT_CHUNK // MXU_WIDTH)]
    e = e_ref[...]
    i1, i2 = _split_expert(e)
    acc = hs_ref[...]
    for g in range(groups):
        lanes = slice((g % per_dot) * PEER_NKEYS, (g % per_dot + 1) * PEER_NKEYS)
        got = jnp.take_along_axis(hd[g // per_dot][:, lanes], i2, axis=1)
        acc = jnp.where(i1 == c * groups + g, got, acc)
    hs_ref[...] = acc


def _peer_up(xn_bf16, e, u_tab):
    n, d = xn_bf16.shape
    n_exp = u_tab.shape[0]
    tn = 1024 if n % 1024 == 0 else (256 if n % 256 == 0 else n)
    return pl.pallas_call(
        _peer_up_kernel, out_shape=jax.ShapeDtypeStruct((n, PEER_SLOTS), F32),
        grid=(n // tn, n_exp // EXPERT_CHUNK),
        in_specs=[pl.BlockSpec((tn, d), lambda i, c: (i, 0)),
                  pl.BlockSpec((tn, PEER_SLOTS), lambda i, c: (i, 0)),
                  pl.BlockSpec((EXPERT_CHUNK, d), lambda i, c: (c, 0))],
        out_specs=pl.BlockSpec((tn, PEER_SLOTS), lambda i, c: (i, 0)),
        compiler_params=_cparams(2), name="peer_up")(xn_bf16, e, u_tab)


TOKEN_BATCH = 2 * SUBLANES
DOWN_EXPERT_CHUNK = 1024


def _peer_down_kernel(tn, e_ref, gate_ref, hs_ref, v_ref, h_ref, gf_ref, y_ref, act_s, a_s):
    c = pl.program_id(1)
    nk = PEER_NKEYS
    groups = DOWN_EXPERT_CHUNK // nk

    @pl.when(c == 0)
    def _():
        y_ref[...] = jnp.zeros(y_ref.shape, F32)
        hv = hs_ref[...]
        gelu = 0.5 * hv * (1.0 + lax.erf(hv * math.sqrt(0.5)))
        act_s[...] = gate_ref[...] * gelu
        sub = lax.broadcasted_iota(I32, (nk, PEER_SLOTS), 0)

        def scatter_token(n):
            e = e_ref[pl.ds(n, 1), :]
            a = act_s[pl.ds(n, 1), :]
            i1, i2 = _split_expert(e)
            pt = jnp.where(sub == i1, a, 0.0).astype(BF16)
            qt = jnp.where(sub == i2, 1.0, 0.0).astype(BF16)
            return lax.dot_general(pt, qt, _NT, preferred_element_type=F32)

        def batch(t, carry):
            n0 = pl.multiple_of(t * TOKEN_BATCH, TOKEN_BATCH)
            halves = []
            for s0 in range(0, TOKEN_BATCH, SUBLANES):
                per_token = jnp.stack([scatter_token(n0 + s0 + k) for k in range(SUBLANES)], axis=0)
                halves.append(jnp.swapaxes(per_token, 0, 1))
            a_s[:, pl.ds(n0, TOKEN_BATCH), :] = jnp.concatenate(halves, axis=1).astype(BF16)
            return carry

        lax.fori_loop(0, tn // TOKEN_BATCH, batch, 0)

    lhs = jnp.concatenate([a_s[c * groups + g] for g in range(groups)], axis=1)
    y_ref[...] += jnp.dot(lhs, v_ref[...], preferred_element_type=F32)

    @pl.when(c == pl.num_programs(1) - 1)
    def _():
        y_ref[...] = _rmsnorm(h_ref[...] + y_ref[...], gf_ref[...])


def _peer_down(e, gate, hs, v_bf16, h, g_final):
    n, d = h.shape
    n_exp = v_bf16.shape[0]
    tn = 512 if n % 512 == 0 else n
    tok = lambda i, c: (i, 0)
    return pl.pallas_call(
        functools.partial(_peer_down_kernel, tn),
        out_shape=jax.ShapeDtypeStruct((n, d), F32), grid=(n // tn, n_exp // DOWN_EXPERT_CHUNK),
        in_specs=[pl.BlockSpec((tn, PEER_SLOTS), tok), pl.BlockSpec((tn, PEER_SLOTS), tok),
                  pl.BlockSpec((tn, PEER_SLOTS), tok),
                  pl.BlockSpec((DOWN_EXPERT_CHUNK, d), lambda i, c: (c, 0)),
                  pl.BlockSpec((tn, d), tok, pipeline_mode=pl.Buffered(1)), _resident((1, d))],
        out_specs=pl.BlockSpec((tn, d), tok),
        scratch_shapes=[pltpu.VMEM((tn, PEER_SLOTS), F32), pltpu.VMEM((PEER_NKEYS, tn, PEER_NKEYS), BF16)],
        compiler_params=_cparams(2), name="peer_down")(
            e, gate, hs, v_bf16, h, g_final.reshape(1, d))


def _peer_and_final_norm(h, xn_bf16, pq, sub_keys, u_tab, v_bf16, g_final):
    n = h.shape[0]
    n_pad = -(-n // ROUTE_TOKENS) * ROUTE_TOKENS
    pq_pad = pq if n_pad == n else jnp.pad(pq, ((0, n_pad - n), (0, 0)))
    e, gate = _route(pq_pad, sub_keys)
    e, gate = e[:n], gate[:n]
    hs = _peer_up(xn_bf16, e, u_tab)
    return _peer_down(e, gate, hs, v_bf16, h, g_final)


def kernel(x_prompt, x_sample, cache_k, cache_v, state_conv, page_table, rel_bias_table, norm_mix_g, w_in,
           w_dw, b_dw, conv_ln_g, conv_ln_b, w_out, norm_ffn_g, peer_wq, peer_sub_keys, peer_u, peer_v,
           final_norm_g):
    batch, seq, d = x_prompt.shape
    bs, s_new, _ = x_sample.shape
    depth = w_in.shape[0]
    assert s_new == 1 and depth == 1 and seq % MOBA_BLOCK == 0 and seq // MOBA_BLOCK >= MOBA_TOPK
    hist = CONV_WIDTH - 1
    blk = MOBA_BLOCK

    dist = np.arange(blk)[None, :] - np.arange(blk)[:, None]
    bkt_prompt = jnp.concatenate([_rel_bucket(dist), _rel_bucket(dist + blk)], axis=0)
    bias_prompt = _bias_lookup(rel_bias_table, bkt_prompt).reshape(N_HEADS, 2, blk, blk)
    bkt_last = jnp.broadcast_to(_rel_bucket(blk - np.arange(blk))[None, :], (SUBLANES, blk))
    bias_last = _bias_lookup(rel_bias_table, bkt_last)[:, 0, :]

    hp = x_prompt.reshape(batch * seq, d)
    hs_ = x_sample.reshape(bs, d)
    l = 0
    w_in_b = w_in[l].astype(BF16)
    w_out_b = w_out[l].astype(BF16)
    wq_b = peer_wq[l].astype(BF16)
    u_b = peer_u[l]
    v_b = peer_v[l].astype(BF16)

    qp, kp, vp, up = _proj(hp, norm_mix_g[l], w_in_b)
    qs, ks, vs, us = _proj(hs_, norm_mix_g[l], w_in_b)
    q3 = qs.reshape(bs, N_HEADS, HEAD_DIM)

    attn_p, top = _prompt_attention_and_sample_gate(qp, kp, vp, bias_prompt, rel_bias_table, batch, seq,
                                                    q3, cache_k[l], page_table)
    conv_p = _conv_prompt(up, w_dw[l], b_dw[l], conv_ln_g[l], conv_ln_b[l], batch, seq)
    h_p, xn_p, pq_p = _mix_out(attn_p, conv_p, hp, w_out_b, norm_ffn_g[l], wq_b)
    y_p = _peer_and_final_norm(h_p, xn_p, pq_p, peer_sub_keys[l], u_b, v_b, final_norm_g)

    attn_s = _sample_attention(q3, ks.reshape(q3.shape), vs.reshape(q3.shape), cache_k[l], cache_v[l],
                               page_table, top, rel_bias_table, bias_last)
    conv_s, state_new = _conv_step(state_conv[l], us, w_dw[l], b_dw[l], conv_ln_g[l], conv_ln_b[l])
    h_s, xn_s, pq_s = _mix_out(attn_s, conv_s, hs_, w_out_b, norm_ffn_g[l], wq_b)
    y_s = _peer_and_final_norm(h_s, xn_s, pq_s, peer_sub_keys[l], u_b, v_b, final_norm_g)

    kv_p = (1, batch, seq, N_HEADS, HEAD_DIM)
    kv_s = (1, bs, 1, N_HEADS, HEAD_DIM)
    conv_prompt_new = up.reshape(batch, seq, -1)[:, seq - hist:, :][None]
    return (y_p.reshape(batch, seq, d), y_s.reshape(bs, 1, d),
            kp.reshape(kv_p), vp.reshape(kv_p), conv_prompt_new,
            ks.reshape(kv_s), vs.reshape(kv_s), state_new[None])
```

```python
import functools
import math

import numpy as np
import jax
import jax.numpy as jnp
from jax import lax
from jax.experimental import pallas as pl
from jax.experimental.pallas import tpu as pltpu

F32, BF16, I32 = jnp.float32, jnp.bfloat16, jnp.int32

N_HEADS = 8
HEAD_DIM = 128
ATTN_WIDTH = N_HEADS * HEAD_DIM
MOBA_BLOCK = 256
MOBA_TOPK = 3
PAGE_SIZE = 128
REL_BUCKETS = 32
REL_MAX_DIST = 128
CONV_WIDTH = 31
PEER_HEADS = 8
PEER_NKEYS = 128
PEER_TOPK = 16
PEER_SLOTS = PEER_HEADS * PEER_TOPK
EPS = 1e-6
NEG = -1e30
SCALE = HEAD_DIM ** -0.5

LANES = 128
SUBLANES = 8
MXU_WIDTH = 256
VMEM_LIMIT_BYTES = 56 * 1024 * 1024

_NT = (((1,), (1,)), ((), ()))


def _cparams(n_axes, vmem_limit_bytes=VMEM_LIMIT_BYTES):
    return pltpu.CompilerParams(dimension_semantics=("arbitrary",) * n_axes,
                                vmem_limit_bytes=vmem_limit_bytes)


def _resident(shape):
    nd = len(shape)
    return pl.BlockSpec(shape, lambda *_: (0,) * nd, pipeline_mode=pl.Buffered(1))


def _rmsnorm(x, g):
    return x * lax.rsqrt(jnp.mean(x * x, axis=-1, keepdims=True) + EPS) * g


def _proj_kernel(x_ref, g_ref, w_ref, q_ref, k_ref, v_ref, u_ref):
    xn = _rmsnorm(x_ref[...], g_ref[...]).astype(BF16)
    c = ATTN_WIDTH

    def mm(j):
        return jnp.dot(xn, w_ref[:, j * c:(j + 1) * c], preferred_element_type=F32)

    q_ref[...] = mm(0)
    k_ref[...] = mm(1)
    v_ref[...] = mm(2)
    u_ref[...] = mm(3) * jax.nn.sigmoid(mm(4))


def _proj(x, g, w_bf16):
    m, d = x.shape
    tm = min(m, 256)
    row = lambda i: (i, 0)
    out = jax.ShapeDtypeStruct((m, ATTN_WIDTH), F32)
    return pl.pallas_call(
        _proj_kernel, out_shape=(out,) * 4, grid=(m // tm,),
        in_specs=[pl.BlockSpec((tm, d), row), _resident((1, d)), _resident(w_bf16.shape)],
        out_specs=(pl.BlockSpec((tm, ATTN_WIDTH), row),) * 4,
        compiler_params=_cparams(1), name="proj")(x, g.reshape(1, d), w_bf16)


MASKED_BUCKET = REL_BUCKETS


def _rel_bucket(dist):
    dist = jnp.asarray(dist, I32)
    n = jnp.maximum(dist, 0)
    max_exact = REL_BUCKETS // 2
    nf = jnp.maximum(n, 1).astype(F32)
    large = max_exact + (jnp.log(nf / max_exact) / math.log(REL_MAX_DIST / max_exact)
                         * (REL_BUCKETS - max_exact)).astype(I32)
    bucket = jnp.where(n < max_exact, n, jnp.minimum(large, REL_BUCKETS - 1))
    return jnp.where(dist < 0, MASKED_BUCKET, bucket)


def _bias_kernel(tab_ref, bkt_ref, o_ref):
    h = pl.program_id(0)
    bkt = bkt_ref[...]
    acc = jnp.full(bkt.shape, NEG, F32)
    for b in range(REL_BUCKETS):
        acc = jnp.where(bkt == b, tab_ref[b, h], acc)
    o_ref[...] = acc


def _bias_lookup(rel_table, bkt):
    r, c = bkt.shape
    return pl.pallas_call(
        _bias_kernel, out_shape=jax.ShapeDtypeStruct((N_HEADS, r, c), F32), grid=(N_HEADS,),
        in_specs=[pl.BlockSpec(memory_space=pltpu.SMEM), _resident((r, c))],
        out_specs=pl.BlockSpec((None, r, c), lambda h: (h, 0, 0)),
        compiler_params=_cparams(1), name="rel_bias")(rel_table, bkt)


ATTN_WIDE_GROUP = 4
PAGES_PER_BLOCK = MOBA_BLOCK // PAGE_SIZE


class _GateStream:
    def __init__(self, q_sample, cache_k, page_table, n_steps, flat_step):
        bs, n_pages = page_table.shape
        assert (bs * n_pages) % n_steps == 0
        self.pages_per_step = bs * n_pages // n_steps
        assert n_pages % self.pages_per_step == 0 and self.pages_per_step % PAGES_PER_BLOCK == 0
        self.steps_per_seq = n_pages // self.pages_per_step
        self.n_blk = n_pages // PAGES_PER_BLOCK
        assert self.n_blk >= MOBA_TOPK
        self.flat_step = flat_step
        self.operands = [q_sample] + [cache_k] * self.pages_per_step
        self.prefetch = page_table
        self.out_shape = [jax.ShapeDtypeStruct((bs, N_HEADS, self.n_blk), F32),
                          jax.ShapeDtypeStruct((bs, N_HEADS, LANES), I32)]

    def _seq(self, idx):
        return self.flat_step(*idx) // self.steps_per_seq

    def in_specs(self):
        def page_spec(r):
            def index(*a):
                f = self.flat_step(*a[:-1])
                return (a[-1][f // self.steps_per_seq, (f % self.steps_per_seq) * self.pages_per_step + r],
                        0, 0, 0)
            return pl.BlockSpec((None, PAGE_SIZE, N_HEADS, HEAD_DIM), index)

        return ([pl.BlockSpec((None, N_HEADS, HEAD_DIM), lambda *a: (self._seq(a[:-1]), 0, 0))]
                + [page_spec(r) for r in range(self.pages_per_step)])

    def out_specs(self):
        per_seq = lambda *a: (self._seq(a[:-1]), 0, 0)
        return [pl.BlockSpec((None, N_HEADS, self.n_blk), per_seq),
                pl.BlockSpec((None, N_HEADS, LANES), per_seq)]

    def start_step(self, grid_idx, g_ref):
        s = lax.rem(self.flat_step(*grid_idx), self.steps_per_seq)

        @pl.when(s == 0)
        def _():
            g_ref[...] = jnp.zeros(g_ref.shape, F32)

        return s

    @staticmethod
    def top_blocks(tops):
        top = jnp.concatenate(tops, axis=0)
        return top[:, :, :MOBA_TOPK].reshape(top.shape[0], N_HEADS * MOBA_TOPK)


def _gate_step(s, n_blk, q_ref, pages, g_ref, top_ref):
    blocks_per_step = len(pages) // PAGES_PER_BLOCK
    q = q_ref[...]
    lane = lax.broadcasted_iota(I32, (N_HEADS, n_blk), 1)
    acc = g_ref[...]
    for p in range(blocks_per_step):
        bsum = functools.reduce(jnp.add, [jnp.sum(pages[PAGES_PER_BLOCK * p + r][...], axis=0)
                                          for r in range(PAGES_PER_BLOCK)])
        gn = jnp.sum(q * (bsum * (1.0 / MOBA_BLOCK)), axis=1, keepdims=True)
        acc = jnp.where(lane == s * blocks_per_step + p, gn, acc)
    g_ref[...] = acc

    g = acc
    out_lane = lax.broadcasted_iota(I32, (N_HEADS, LANES), 1)
    top = jnp.zeros((N_HEADS, LANES), I32)
    for t in range(MOBA_TOPK):
        m = jnp.max(g, axis=1, keepdims=True)
        idx = jnp.min(jnp.where(g == m, lane, n_blk), axis=1, keepdims=True)
        top = jnp.where(out_lane == t, idx, top)
        g = jnp.where(lane == idx, -jnp.inf, g)
    top_ref[...] = top


def _attn_kernel(nb, tab_ref, q_ref, k_ref, v_ref, bias_ref, o_ref,
                 kb_s, vt_s, km_s, far_s, m_s, l_s, acc_s):
    h = pl.program_id(1)
    i = pl.program_id(2)
    blk = MOBA_BLOCK

    @pl.when(i == 0)
    def _():
        kb_s[...] = k_ref[...].astype(BF16)
        for j in range(nb):
            rows = slice(j * blk, (j + 1) * blk)
            km_s[j:j + 1, :] = jnp.mean(k_ref[rows, :], axis=0, keepdims=True)
            vt_s[:, rows] = v_ref[rows, :].T.astype(BF16)

    q = q_ref[...]
    qt = q.T.astype(BF16)

    gate = lax.dot_general(km_s[...], q, _NT, precision=lax.Precision.HIGHEST,
                           preferred_element_type=F32)
    sub = lax.broadcasted_iota(I32, (nb, blk), 0)
    gate = jnp.where(sub < i, gate, NEG)
    rank = jnp.zeros((nb, blk), I32)
    for jp in range(nb):
        gj = gate[jp:jp + 1, :]
        rank = rank + jnp.where(gj == gate, jnp.where(jp < sub, 1, 0),
                                jnp.where(gj > gate, 1, 0))
    selneg = jnp.where((rank < MOBA_TOPK) & (sub < i), 0.0, NEG)
    far = selneg + tab_ref[REL_BUCKETS - 1, h]

    def row_of(x, j):
        return jnp.where(j >= 0, jnp.sum(jnp.where(sub == j, x, 0.0), axis=0, keepdims=True), NEG)

    n_first = ATTN_WIDE_GROUP
    n_rest = jnp.maximum(i + 1 - n_first, 0)
    far_s[...] = jnp.where(sub < n_rest, far, NEG)

    def partial_softmax(blocks):
        rows = [pl.multiple_of(j * blk, blk) for j, _ in blocks]
        sts = [jnp.dot(kb_s[pl.ds(r, blk), :], qt, preferred_element_type=F32) * SCALE + extra
               for r, (_, extra) in zip(rows, blocks)]
        ms = [jnp.max(st, axis=0, keepdims=True) for st in sts]
        ps = [jnp.exp(st - m) for st, m in zip(sts, ms)]
        ls = [jnp.sum(p, axis=0, keepdims=True) for p in ps]
        os_ = [jnp.dot(vt_s[:, pl.ds(r, blk)], p.astype(BF16), preferred_element_type=F32)
               for r, p in zip(rows, ps)]
        return list(zip(ms, ls, os_))

    def merge(parts):
        m = functools.reduce(jnp.maximum, [pm for pm, _, _ in parts])
        w = [jnp.exp(pm - m) for pm, _, _ in parts]
        l = functools.reduce(jnp.add, [wi * pl_ for wi, (_, pl_, _) in zip(w, parts)])
        o = functools.reduce(jnp.add, [wi * po for wi, (_, _, po) in zip(w, parts)])
        return m, l, o

    first = [(i, bias_ref[0]), (jnp.maximum(i - 1, 0), bias_ref[1] + row_of(selneg, i - 1))]
    first += [(jnp.maximum(i - back, 0), row_of(far, i - back)) for back in range(2, n_first)]
    m0, l0, o0 = merge(partial_softmax(first))
    m_s[...] = m0
    l_s[...] = l0
    acc_s[...] = o0

    def group(g, carry):
        j0 = g * ATTN_WIDE_GROUP
        parts = partial_softmax([(j0 + u, far_s[pl.ds(j0 + u, 1), :]) for u in range(ATTN_WIDE_GROUP)])
        m, l, o = merge([(m_s[...], l_s[...], acc_s[...])] + parts)
        m_s[...] = m
        l_s[...] = l
        acc_s[...] = o
        return carry

    lax.fori_loop(0, (n_rest + ATTN_WIDE_GROUP - 1) // ATTN_WIDE_GROUP, group, 0)

    o_ref[...] = (acc_s[...] / l_s[...]).T


def _prompt_attention(q, k, v, bias, rel_table, batch, seq):
    nb = seq // MOBA_BLOCK
    blk = MOBA_BLOCK
    qo = lambda b, h, i: (b * nb + i, h)
    kv = lambda b, h, i: (b, h)
    return pl.pallas_call(
        functools.partial(_attn_kernel, nb),
        out_shape=jax.ShapeDtypeStruct((batch * seq, ATTN_WIDTH), F32),
        grid=(batch, N_HEADS, nb),
        in_specs=[pl.BlockSpec(memory_space=pltpu.SMEM),
                  pl.BlockSpec((blk, HEAD_DIM), qo),
                  pl.BlockSpec((seq, HEAD_DIM), kv),
                  pl.BlockSpec((seq, HEAD_DIM), kv),
                  pl.BlockSpec((None, 2, blk, blk), lambda b, h, i: (h, 0, 0, 0))],
        out_specs=pl.BlockSpec((blk, HEAD_DIM), qo),
        scratch_shapes=[pltpu.VMEM((seq, HEAD_DIM), BF16), pltpu.VMEM((HEAD_DIM, seq), BF16),
                        pltpu.VMEM((nb, HEAD_DIM), F32), pltpu.VMEM((nb, blk), F32),
                        pltpu.VMEM((1, blk), F32), pltpu.VMEM((1, blk), F32),
                        pltpu.VMEM((HEAD_DIM, blk), F32)],
        compiler_params=_cparams(3), name="moba_prompt")(rel_table, q, k, v, bias)


PAGES_PER_HEAD = MOBA_TOPK * PAGES_PER_BLOCK
PAGES_PER_SEQ = N_HEADS * PAGES_PER_HEAD


def _sattn_kernel(n_blk, pt_ref, top_ref, tab_ref, q_ref, kn_ref, vn_ref, b_last_ref, ck_hbm, cv_hbm,
                  o_ref, kbuf, vbuf, ksem, vsem):
    b = pl.program_id(0)
    slot = lax.rem(b, 2)

    def page_copies(seq, sl):
        out = []
        for h in range(N_HEADS):
            for t in range(MOBA_TOPK):
                blk = top_ref[seq, h * MOBA_TOPK + t]
                for r in range(PAGES_PER_BLOCK):
                    page = pt_ref[seq, PAGES_PER_BLOCK * blk + r]
                    idx = h * PAGES_PER_HEAD + t * PAGES_PER_BLOCK + r
                    out.append(pltpu.make_async_copy(ck_hbm.at[page, :, h, :], kbuf.at[sl, idx], ksem.at[sl]))
                    out.append(pltpu.make_async_copy(cv_hbm.at[page, :, h, :], vbuf.at[sl, idx], vsem.at[sl]))
        return out

    @pl.when(b == 0)
    def _():
        for cp in page_copies(0, 0):
            cp.start()

    @pl.when(b + 1 < pl.num_programs(0))
    def _():
        for cp in page_copies(b + 1, 1 - slot):
            cp.start()

    for cp in page_copies(b, slot):
        cp.wait()

    q = q_ref[...]
    vn = vn_ref[...]
    s_self = jnp.sum(q * kn_ref[...], axis=1, keepdims=True) * SCALE
    sub = lax.broadcasted_iota(I32, (N_HEADS, HEAD_DIM), 0)
    out = jnp.zeros((N_HEADS, HEAD_DIM), F32)
    n_keys = PAGES_PER_HEAD * PAGE_SIZE
    for h in range(N_HEADS):
        pages = slice(h * PAGES_PER_HEAD, (h + 1) * PAGES_PER_HEAD)
        kh = kbuf[slot, pages].reshape(n_keys, HEAD_DIM).astype(BF16)
        vh = vbuf[slot, pages].reshape(n_keys, HEAD_DIM).astype(BF16)
        q8 = jnp.broadcast_to(q[h:h + 1, :], (SUBLANES, HEAD_DIM)).astype(BF16)
        s = lax.dot_general(q8, kh, _NT, preferred_element_type=F32)[:1, :] * SCALE
        far_bias = tab_ref[REL_BUCKETS - 1, h]
        bias = []
        for t in range(MOBA_TOPK):
            is_last = top_ref[b, h * MOBA_TOPK + t] == n_blk - 1
            bias.append(jnp.where(is_last, b_last_ref[h:h + 1, :], far_bias))
        s = s + jnp.concatenate(bias, axis=1)
        sh = s_self[h:h + 1, :] + tab_ref[0, h]
        m = jnp.maximum(jnp.max(s, axis=1, keepdims=True), sh)
        p = jnp.exp(s - m)
        p_self = jnp.exp(sh - m)
        l = jnp.sum(p, axis=1, keepdims=True) + p_self
        p8 = jnp.broadcast_to(p, (SUBLANES, n_keys)).astype(BF16)
        o = jnp.dot(p8, vh, preferred_element_type=F32)[:1, :] + p_self * vn[h:h + 1, :]
        out = jnp.where(sub == h, o / l, out)
    o_ref[...] = out


def _sample_attention(q3, kn3, vn3, cache_k, cache_v, page_table, top, rel_table, bias_last):
    bs = q3.shape[0]
    n_blk = page_table.shape[1] * PAGE_SIZE // MOBA_BLOCK
    per_b = pl.BlockSpec((None, N_HEADS, HEAD_DIM), lambda b, pt, tp: (b, 0, 0))
    buf = pltpu.VMEM((2, PAGES_PER_SEQ, PAGE_SIZE, HEAD_DIM), F32)
    grid_spec = pltpu.PrefetchScalarGridSpec(
        num_scalar_prefetch=2, grid=(bs,),
        in_specs=[pl.BlockSpec(memory_space=pltpu.SMEM), per_b, per_b, per_b,
                  pl.BlockSpec((N_HEADS, MOBA_BLOCK), lambda b, pt, tp: (0, 0)),
                  pl.BlockSpec(memory_space=pl.ANY), pl.BlockSpec(memory_space=pl.ANY)],
        out_specs=per_b,
        scratch_shapes=[buf, buf, pltpu.SemaphoreType.DMA((2,)), pltpu.SemaphoreType.DMA((2,))])
    out = pl.pallas_call(
        functools.partial(_sattn_kernel, n_blk),
        out_shape=jax.ShapeDtypeStruct((bs, N_HEADS, HEAD_DIM), F32),
        grid_spec=grid_spec, compiler_params=_cparams(1), name="moba_sample")(
            page_table, top, rel_table, q3, kn3, vn3, bias_last, cache_k, cache_v)
    return out.reshape(bs, ATTN_WIDTH)


CONV_HALO = 32
CONV_ROWS = 32


def _ln_swish(y, g, b):
    mu = jnp.mean(y, axis=-1, keepdims=True)
    var = jnp.mean(jnp.square(y - mu), axis=-1, keepdims=True)
    yn = (y - mu) * lax.rsqrt(var + EPS) * g + b
    return yn * jax.nn.sigmoid(yn)


def _conv_kernel(tt, cur_ref, halo_ref, w_ref, b_ref, lg_ref, lb_ref, y_ref, ext_s, yc_s):
    t = pl.program_id(1)
    ch = cur_ref.shape[1]
    ext_s[CONV_HALO:, :] = cur_ref[...]

    @pl.when(t == 0)
    def _():
        ext_s[:CONV_HALO, :] = jnp.zeros((CONV_HALO, ch), F32)

    @pl.when(t > 0)
    def _():
        ext_s[:CONV_HALO, :] = halo_ref[...]

    lead = CONV_HALO - (CONV_WIDTH - 1)
    rc = CONV_ROWS

    def chunk(r, carry):
        r0 = pl.multiple_of(r * rc, rc)
        for lt in range(ch // LANES):
            ls = slice(lt * LANES, (lt + 1) * LANES)
            win = ext_s[pl.ds(r0, rc + CONV_HALO), ls]
            acc = jnp.zeros((rc, LANES), F32)
            for sh in range(SUBLANES):
                ws = win if sh == 0 else win[sh:sh + rc + CONV_HALO - SUBLANES]
                for a in range(CONV_HALO // SUBLANES + 1):
                    k = SUBLANES * a + sh - lead
                    if 0 <= k < CONV_WIDTH and SUBLANES * a + rc <= ws.shape[0]:
                        acc = acc + w_ref[k:k + 1, ls] * ws[SUBLANES * a:SUBLANES * a + rc]
            yc_s[pl.ds(r0, rc), ls] = acc + b_ref[:, ls]
        return carry

    lax.fori_loop(0, tt // rc, chunk, 0)
    y_ref[...] = _ln_swish(yc_s[...], lg_ref[...], lb_ref[...])


def _conv_prompt(u, w_dw, b_dw, ln_g, ln_b, batch, seq):
    ch = u.shape[1]
    tt = min(seq, 512)
    nt = seq // tt
    hpt = tt // CONV_HALO
    vec = lambda x: x.reshape(1, ch)
    return pl.pallas_call(
        functools.partial(_conv_kernel, tt),
        out_shape=jax.ShapeDtypeStruct((batch * seq, ch), F32), grid=(batch, nt),
        in_specs=[pl.BlockSpec((tt, ch), lambda b, t: (b * nt + t, 0)),
                  pl.BlockSpec((CONV_HALO, ch), lambda b, t: (jnp.maximum((b * nt + t) * hpt - 1, 0), 0)),
                  _resident(w_dw.shape), _resident((1, ch)), _resident((1, ch)), _resident((1, ch))],
        out_specs=pl.BlockSpec((tt, ch), lambda b, t: (b * nt + t, 0)),
        scratch_shapes=[pltpu.VMEM((tt + CONV_HALO, ch), F32), pltpu.VMEM((tt, ch), F32)],
        compiler_params=_cparams(2), name="conv_prompt")(u, u, w_dw, vec(b_dw), vec(ln_g), vec(ln_b))


def _conv_step_kernel(st_ref, u_ref, w_ref, b_ref, lg_ref, lb_ref, y_ref, ns_ref):
    hist = CONV_WIDTH - 1
    u = u_ref[...]
    acc = w_ref[hist:hist + 1, :] * u
    for k in range(hist):
        acc = acc + w_ref[k:k + 1, :] * st_ref[:, k, :]
    y_ref[...] = _ln_swish(acc + b_ref[...], lg_ref[...], lb_ref[...])
    for k in range(hist - 1):
        ns_ref[:, k, :] = st_ref[:, k + 1, :]
    ns_ref[:, hist - 1, :] = u


def _conv_step(state, u, w_dw, b_dw, ln_g, ln_b):
    bs, hist, ch = state.shape
    vec = lambda x: x.reshape(1, ch)
    return pl.pallas_call(
        _conv_step_kernel,
        out_shape=(jax.ShapeDtypeStruct((bs, ch), F32), jax.ShapeDtypeStruct((bs, hist, ch), F32)),
        compiler_params=pltpu.CompilerParams(vmem_limit_bytes=VMEM_LIMIT_BYTES),
        name="conv_step")(state, u, w_dw, vec(b_dw), vec(ln_g), vec(ln_b))


def _mix_kernel(attn_ref, conv_ref, x_ref, wo_ref, g_ref, wq_ref, h_ref, xn_ref, pq_ref):
    mix = jnp.concatenate([attn_ref[...], conv_ref[...]], axis=1).astype(BF16)
    h = x_ref[...] + jnp.dot(mix, wo_ref[...], preferred_element_type=F32)
    h_ref[...] = h
    xn = _rmsnorm(h, g_ref[...]).astype(BF16)
    xn_ref[...] = xn
    pq_ref[...] = jnp.dot(xn, wq_ref[...], preferred_element_type=F32)


def _mix_out(attn, conv, x, wo_bf16, g, wq_bf16):
    m, d = x.shape
    tm = min(m, 256)
    row = lambda i: (i, 0)
    half = attn.shape[1]
    return pl.pallas_call(
        _mix_kernel,
        out_shape=(jax.ShapeDtypeStruct((m, d), F32), jax.ShapeDtypeStruct((m, d), BF16),
                   jax.ShapeDtypeStruct((m, wq_bf16.shape[1]), F32)),
        grid=(m // tm,),
        in_specs=[pl.BlockSpec((tm, half), row), pl.BlockSpec((tm, half), row), pl.BlockSpec((tm, d), row),
                  _resident(wo_bf16.shape), _resident((1, d)), _resident(wq_bf16.shape)],
        out_specs=(pl.BlockSpec((tm, d), row), pl.BlockSpec((tm, d), row),
                   pl.BlockSpec((tm, wq_bf16.shape[1]), row)),
        compiler_params=_cparams(1), name="mix_out")(attn, conv, x, wo_bf16, g.reshape(1, d), wq_bf16)


ROUTE_TOKENS = LANES
_BIG = 1 << 20


def _candidate_layout():
    k = PEER_TOPK
    groups, flat = [], []
    for a in range(k // 2):
        nb = k // (a + 1)
        rows = -(-nb // SUBLANES) * SUBLANES
        groups.append((a, 1, rows))
        flat += [a * k + b if b < nb else _BIG for b in range(rows)]
    groups.append((k // 2, k // 2, 1))
    flat += [a * k for a in range(k // 2, k)]
    return groups, np.asarray(flat, np.int32)


def _route_kernel(pq_ref, sk_ref, flat_ref, e_ref, g_ref, et_s, gt_s):
    k = PEER_TOPK
    nk = PEER_NKEYS
    tn = ROUTE_TOKENS
    row = lax.broadcasted_iota(I32, (nk, tn), 0)
    groups, _ = _candidate_layout()
    flat = flat_ref[...]

    def half_topk(h, c):
        off = pl.multiple_of((h * 2 + c) * nk, nk)
        s = lax.dot_general(sk_ref[h, c], pq_ref[:, pl.ds(off, nk)], _NT,
                            precision=lax.Precision.HIGHEST, preferred_element_type=F32)
        ts, ti = [], []
        for _ in range(k):
            m = jnp.max(s, axis=0, keepdims=True)
            idx = jnp.min(jnp.where(s == m, row, nk), axis=0, keepdims=True)
            ts.append(m)
            ti.append(idx)
            s = jnp.where(row == idx, -jnp.inf, s)
        return jnp.concatenate(ts, axis=0), jnp.concatenate(ti, axis=0)

    def head(h, carry):
        s0, i0 = half_topk(h, 0)
        s1, i1 = half_topk(h, 1)
        cs, ce = [], []
        for a0, na, nb in groups:
            if na == 1:
                cs.append(s0[a0:a0 + 1] + s1[:nb])
                ce.append(i0[a0:a0 + 1] * nk + i1[:nb])
            else:
                cs.append(s0[a0:a0 + na] + s1[:1])
                ce.append(i0[a0:a0 + na] * nk + i1[:1])
        cand = jnp.where(flat < _BIG, jnp.concatenate(cs, axis=0), -jnp.inf)
        cexp = jnp.concatenate(ce, axis=0)
        bs, be = [], []
        for _ in range(k):
            m = jnp.max(cand, axis=0, keepdims=True)
            idx = jnp.min(jnp.where(cand == m, flat, _BIG), axis=0, keepdims=True)
            hit = flat == idx
            bs.append(m)
            be.append(jnp.max(jnp.where(hit, cexp, -1), axis=0, keepdims=True))
            cand = jnp.where(hit, -jnp.inf, cand)
        best = jnp.concatenate(bs, axis=0)
        p = jnp.exp(best - best[:1])
        r = pl.multiple_of(h * k, k)
        gt_s[pl.ds(r, k), :] = p / jnp.sum(p, axis=0, keepdims=True)
        et_s[pl.ds(r, k), :] = jnp.concatenate(be, axis=0)
        return carry

    lax.fori_loop(0, PEER_HEADS, head, 0, unroll=2)
    e_ref[...] = et_s[...].T
    g_ref[...] = gt_s[...].T


def _route(pq, sub_keys):
    n, qd = pq.shape
    tn = ROUTE_TOKENS
    _, flat = _candidate_layout()
    flat = jnp.asarray(np.broadcast_to(flat[:, None], (flat.shape[0], tn)))
    row = lambda i: (i, 0)
    return pl.pallas_call(
        _route_kernel,
        out_shape=(jax.ShapeDtypeStruct((n, PEER_SLOTS), I32), jax.ShapeDtypeStruct((n, PEER_SLOTS), F32)),
        grid=(n // tn,),
        in_specs=[pl.BlockSpec((tn, qd), row), _resident(sub_keys.shape), _resident(flat.shape)],
        out_specs=(pl.BlockSpec((tn, PEER_SLOTS), row), pl.BlockSpec((tn, PEER_SLOTS), row)),
        scratch_shapes=[pltpu.VMEM((PEER_SLOTS, tn), I32), pltpu.VMEM((PEER_SLOTS, tn), F32)],
        compiler_params=_cparams(1), name="peer_route")(pq, sub_keys, flat)


EXPERT_CHUNK = 1024
_NKEYS_BITS = PEER_NKEYS.bit_length() - 1
assert 1 << _NKEYS_BITS == PEER_NKEYS


def _split_expert(e):
    return lax.shift_right_logical(e, _NKEYS_BITS), e & (PEER_NKEYS - 1)


def _host_call(kernel_fn, name, grid, in_specs, out_specs, out_shape, scratch_shapes, operands, gate_args):
    n_chunks = grid[1]
    stream = None
    if gate_args is not None:
        stream = _GateStream(*gate_args, n_steps=grid[0] * n_chunks, flat_step=lambda i, c: i * n_chunks + c)
        in_specs = in_specs + stream.in_specs()
        out_specs = out_specs + stream.out_specs()
        out_shape = out_shape + stream.out_shape
        operands = [stream.prefetch] + operands + stream.operands
    grid_spec = pltpu.PrefetchScalarGridSpec(
        num_scalar_prefetch=0 if stream is None else 1, grid=grid, in_specs=in_specs, out_specs=out_specs,
        scratch_shapes=scratch_shapes)
    return pl.pallas_call(functools.partial(kernel_fn, stream), out_shape=out_shape, grid_spec=grid_spec,
                          compiler_params=_cparams(2), name=name)(*operands)


def _split_host_refs(stream, refs, n_in, n_out):
    if stream is None:
        return refs[:n_in], refs[n_in:n_in + n_out], refs[n_in + n_out:], None
    refs = refs[1:]
    n_gate_in = 1 + stream.pages_per_step
    gate_in = refs[n_in:n_in + n_gate_in]
    outs = refs[n_in + n_gate_in:]
    gate = (gate_in[0], gate_in[1:], outs[n_out], outs[n_out + 1])
    return refs[:n_in], outs[:n_out], outs[n_out + 2:], gate


def _peer_up_kernel(stream, *refs):
    (xn_ref, e_ref, u_ref), (hs_ref,), _, gate = _split_host_refs(stream, refs, 3, 1)
    c = pl.program_id(1)
    groups = EXPERT_CHUNK // PEER_NKEYS
    if gate is not None:
        gate_s = stream.start_step((pl.program_id(0), c), gate[2])

    @pl.when(c == 0)
    def _():
        hs_ref[...] = jnp.zeros(hs_ref.shape, F32)

    if gate is not None:
        _gate_step(gate_s, stream.n_blk, *gate)

    xn = xn_ref[...]
    per_dot = MXU_WIDTH // PEER_NKEYS
    hd = [lax.dot_general(xn, u_ref[j * MXU_WIDTH:(j + 1) * MXU_WIDTH, :].astype(BF16), _NT,
                          preferred_element_type=F32) for j in range(EXPERT_CHUNK // MXU_WIDTH)]
    e = e_ref[...]
    i1, i2 = _split_expert(e)
    acc = hs_ref[...]
    for g in range(groups):
        lanes = slice((g % per_dot) * PEER_NKEYS, (g % per_dot + 1) * PEER_NKEYS)
        got = jnp.take_along_axis(hd[g // per_dot][:, lanes], i2, axis=1)
        acc = jnp.where(i1 == c * groups + g, got, acc)
    hs_ref[...] = acc


def _peer_up(xn_bf16, e, u_tab, gate_args=None):
    n, d = xn_bf16.shape
    n_exp = u_tab.shape[0]
    tn = 1024 if n % 1024 == 0 else (256 if n % 256 == 0 else n)
    tok = lambda i, c, *_: (i, 0)
    return _host_call(
        _peer_up_kernel, "peer_up", (n // tn, n_exp // EXPERT_CHUNK),
        [pl.BlockSpec((tn, d), tok), pl.BlockSpec((tn, PEER_SLOTS), tok),
         pl.BlockSpec((EXPERT_CHUNK, d), lambda i, c, *_: (c, 0))],
        [pl.BlockSpec((tn, PEER_SLOTS), tok)], [jax.ShapeDtypeStruct((n, PEER_SLOTS), F32)], [],
        [xn_bf16, e, u_tab], gate_args)


TOKEN_BATCH = 2 * SUBLANES
DOWN_EXPERT_CHUNK = 1024


def _peer_down_kernel(tn, stream, *refs):
    (e_ref, gate_ref, hs_ref, v_ref, h_ref, gf_ref), (y_ref,), (act_s, a_s), block_gate = _split_host_refs(
        stream, refs, 6, 1)
    c = pl.program_id(1)
    nk = PEER_NKEYS
    groups = DOWN_EXPERT_CHUNK // nk
    if block_gate is not None:
        gate_s = stream.start_step((pl.program_id(0), c), block_gate[2])

    @pl.when(c == 0)
    def _():
        y_ref[...] = jnp.zeros(y_ref.shape, F32)
        hv = hs_ref[...]
        gelu = 0.5 * hv * (1.0 + lax.erf(hv * math.sqrt(0.5)))
        act_s[...] = gate_ref[...] * gelu
        sub = lax.broadcasted_iota(I32, (nk, PEER_SLOTS), 0)

        def scatter_token(n):
            e = e_ref[pl.ds(n, 1), :]
            a = act_s[pl.ds(n, 1), :]
            i1, i2 = _split_expert(e)
            pt = jnp.where(sub == i1, a, 0.0).astype(BF16)
            qt = jnp.where(sub == i2, 1.0, 0.0).astype(BF16)
            return lax.dot_general(pt, qt, _NT, preferred_element_type=F32)

        def batch(t, carry):
            n0 = pl.multiple_of(t * TOKEN_BATCH, TOKEN_BATCH)
            halves = []
            for s0 in range(0, TOKEN_BATCH, SUBLANES):
                per_token = jnp.stack([scatter_token(n0 + s0 + k) for k in range(SUBLANES)], axis=0)
                halves.append(jnp.swapaxes(per_token, 0, 1))
            a_s[:, pl.ds(n0, TOKEN_BATCH), :] = jnp.concatenate(halves, axis=1).astype(BF16)
            return carry

        lax.fori_loop(0, tn // TOKEN_BATCH, batch, 0)

    if block_gate is not None:
        _gate_step(gate_s, stream.n_blk, *block_gate)

    lhs = jnp.concatenate([a_s[c * groups + g] for g in range(groups)], axis=1)
    y_ref[...] += jnp.dot(lhs, v_ref[...], preferred_element_type=F32)

    @pl.when(c == pl.num_programs(1) - 1)
    def _():
        y_ref[...] = _rmsnorm(h_ref[...] + y_ref[...], gf_ref[...])


def _peer_down(e, gate, hs, v_bf16, h, g_final, gate_args=None):
    n, d = h.shape
    n_exp = v_bf16.shape[0]
    tn = 512 if n % 512 == 0 else n
    tok = lambda i, c, *_: (i, 0)
    return _host_call(
        functools.partial(_peer_down_kernel, tn), "peer_down", (n // tn, n_exp // DOWN_EXPERT_CHUNK),
        [pl.BlockSpec((tn, PEER_SLOTS), tok), pl.BlockSpec((tn, PEER_SLOTS), tok),
         pl.BlockSpec((tn, PEER_SLOTS), tok),
         pl.BlockSpec((DOWN_EXPERT_CHUNK, d), lambda i, c, *_: (c, 0)),
         pl.BlockSpec((tn, d), tok, pipeline_mode=pl.Buffered(1)),
         pl.BlockSpec((1, d), lambda i, c, *_: (0, 0), pipeline_mode=pl.Buffered(1))],
        [pl.BlockSpec((tn, d), tok)], [jax.ShapeDtypeStruct((n, d), F32)],
        [pltpu.VMEM((tn, PEER_SLOTS), F32), pltpu.VMEM((PEER_NKEYS, tn, PEER_NKEYS), BF16)],
        [e, gate, hs, v_bf16, h, g_final.reshape(1, d)], gate_args)


def _peer_and_final_norm(h, xn_bf16, pq, sub_keys, u_tab, v_bf16, g_final, gate_args=None):
    n = h.shape[0]
    n_pad = -(-n // ROUTE_TOKENS) * ROUTE_TOKENS
    pq_pad = pq if n_pad == n else jnp.pad(pq, ((0, n_pad - n), (0, 0)))
    e, gate = _route(pq_pad, sub_keys)
    e, gate = e[:n], gate[:n]
    if gate_args is None:
        (hs,) = _peer_up(xn_bf16, e, u_tab)
        (y,) = _peer_down(e, gate, hs, v_bf16, h, g_final)
        return y, None
    q_sample, cache_k, page_table = gate_args
    half = q_sample.shape[0] // 2
    hs, _, top_a = _peer_up(xn_bf16, e, u_tab, (q_sample[:half], cache_k, page_table[:half]))
    y, _, top_b = _peer_down(e, gate, hs, v_bf16, h, g_final, (q_sample[half:], cache_k, page_table[half:]))
    return y, _GateStream.top_blocks([top_a, top_b])


def kernel(x_prompt, x_sample, cache_k, cache_v, state_conv, page_table, rel_bias_table, norm_mix_g, w_in,
           w_dw, b_dw, conv_ln_g, conv_ln_b, w_out, norm_ffn_g, peer_wq, peer_sub_keys, peer_u, peer_v,
           final_norm_g):
    batch, seq, d = x_prompt.shape
    bs, s_new, _ = x_sample.shape
    depth = w_in.shape[0]
    assert s_new == 1 and depth == 1 and seq % MOBA_BLOCK == 0 and seq // MOBA_BLOCK >= MOBA_TOPK
    hist = CONV_WIDTH - 1
    blk = MOBA_BLOCK

    dist = np.arange(blk)[None, :] - np.arange(blk)[:, None]
    bkt_prompt = jnp.concatenate([_rel_bucket(dist), _rel_bucket(dist + blk)], axis=0)
    bias_prompt = _bias_lookup(rel_bias_table, bkt_prompt).reshape(N_HEADS, 2, blk, blk)
    bkt_last = jnp.broadcast_to(_rel_bucket(blk - np.arange(blk))[None, :], (SUBLANES, blk))
    bias_last = _bias_lookup(rel_bias_table, bkt_last)[:, 0, :]

    hp = x_prompt.reshape(batch * seq, d)
    hs_ = x_sample.reshape(bs, d)
    l = 0
    w_in_b = w_in[l].astype(BF16)
    w_out_b = w_out[l].astype(BF16)
    wq_b = peer_wq[l].astype(BF16)
    u_b = peer_u[l]
    v_b = peer_v[l].astype(BF16)

    qp, kp, vp, up = _proj(hp, norm_mix_g[l], w_in_b)
    qs, ks, vs, us = _proj(hs_, norm_mix_g[l], w_in_b)
    q3 = qs.reshape(bs, N_HEADS, HEAD_DIM)

    attn_p = _prompt_attention(qp, kp, vp, bias_prompt, rel_bias_table, batch, seq)
    conv_p = _conv_prompt(up, w_dw[l], b_dw[l], conv_ln_g[l], conv_ln_b[l], batch, seq)
    h_p, xn_p, pq_p = _mix_out(attn_p, conv_p, hp, w_out_b, norm_ffn_g[l], wq_b)
    y_p, top = _peer_and_final_norm(h_p, xn_p, pq_p, peer_sub_keys[l], u_b, v_b, final_norm_g,
                                    (q3, cache_k[l], page_table))

    attn_s = _sample_attention(q3, ks.reshape(q3.shape), vs.reshape(q3.shape), cache_k[l], cache_v[l],
                               page_table, top, rel_bias_table, bias_last)
    conv_s, state_new = _conv_step(state_conv[l], us, w_dw[l], b_dw[l], conv_ln_g[l], conv_ln_b[l])
    h_s, xn_s, pq_s = _mix_out(attn_s, conv_s, hs_, w_out_b, norm_ffn_g[l], wq_b)
    y_s, _ = _peer_and_final_norm(h_s, xn_s, pq_s, peer_sub_keys[l], u_b, v_b, final_norm_g)

    kv_p = (1, batch, seq, N_HEADS, HEAD_DIM)
    kv_s = (1, bs, 1, N_HEADS, HEAD_DIM)
    conv_prompt_new = up.reshape(batch, seq, -1)[:, seq - hist:, :][None]
    return (y_p.reshape(batch, seq, d), y_s.reshape(bs, 1, d),
            kp.reshape(kv_p), vp.reshape(kv_p), conv_prompt_new,
            ks.reshape(kv_s), vs.reshape(kv_s), state_new[None])
```

```python
import functools
import math

import numpy as np
import jax
import jax.numpy as jnp
from jax import lax
from jax.experimental import pallas as pl
from jax.experimental.pallas import tpu as pltpu

F32, BF16, I32 = jnp.float32, jnp.bfloat16, jnp.int32

N_HEADS = 8
HEAD_DIM = 128
ATTN_WIDTH = N_HEADS * HEAD_DIM
MOBA_BLOCK = 256
MOBA_TOPK = 3
PAGE_SIZE = 128
REL_BUCKETS = 32
REL_MAX_DIST = 128
CONV_WIDTH = 31
PEER_HEADS = 8
PEER_NKEYS = 128
PEER_TOPK = 16
PEER_SLOTS = PEER_HEADS * PEER_TOPK
EPS = 1e-6
NEG = -1e30
SCALE = HEAD_DIM ** -0.5

LANES = 128
SUBLANES = 8
MXU_WIDTH = 256
VMEM_LIMIT_BYTES = 56 * 1024 * 1024
HOST_VMEM_LIMIT_BYTES = 60 * 1024 * 1024

_NT = (((1,), (1,)), ((), ()))


def _cparams(n_axes, vmem_limit_bytes=VMEM_LIMIT_BYTES):
    return pltpu.CompilerParams(dimension_semantics=("arbitrary",) * n_axes,
                                vmem_limit_bytes=vmem_limit_bytes)


def _resident(shape):
    nd = len(shape)
    return pl.BlockSpec(shape, lambda *_: (0,) * nd, pipeline_mode=pl.Buffered(1))


def _rmsnorm(x, g):
    return x * lax.rsqrt(jnp.mean(x * x, axis=-1, keepdims=True) + EPS) * g


def _proj_kernel(x_ref, g_ref, w_ref, q_ref, k_ref, v_ref, u_ref):
    xn = _rmsnorm(x_ref[...], g_ref[...]).astype(BF16)
    c = ATTN_WIDTH

    def mm(j):
        return jnp.dot(xn, w_ref[:, j * c:(j + 1) * c], preferred_element_type=F32)

    q_ref[...] = mm(0)
    k_ref[...] = mm(1)
    v_ref[...] = mm(2)
    u_ref[...] = mm(3) * jax.nn.sigmoid(mm(4))


def _proj(x, g, w_bf16):
    m, d = x.shape
    tm = min(m, 256)
    row = lambda i: (i, 0)
    out = jax.ShapeDtypeStruct((m, ATTN_WIDTH), F32)
    return pl.pallas_call(
        _proj_kernel, out_shape=(out,) * 4, grid=(m // tm,),
        in_specs=[pl.BlockSpec((tm, d), row), _resident((1, d)), _resident(w_bf16.shape)],
        out_specs=(pl.BlockSpec((tm, ATTN_WIDTH), row),) * 4,
        compiler_params=_cparams(1), name="proj")(x, g.reshape(1, d), w_bf16)


MASKED_BUCKET = REL_BUCKETS


def _rel_bucket(dist):
    dist = jnp.asarray(dist, I32)
    n = jnp.maximum(dist, 0)
    max_exact = REL_BUCKETS // 2
    nf = jnp.maximum(n, 1).astype(F32)
    large = max_exact + (jnp.log(nf / max_exact) / math.log(REL_MAX_DIST / max_exact)
                         * (REL_BUCKETS - max_exact)).astype(I32)
    bucket = jnp.where(n < max_exact, n, jnp.minimum(large, REL_BUCKETS - 1))
    return jnp.where(dist < 0, MASKED_BUCKET, bucket)


def _bias_kernel(tab_ref, bkt_ref, o_ref):
    h = pl.program_id(0)
    bkt = bkt_ref[...]
    acc = jnp.full(bkt.shape, NEG, F32)
    for b in range(REL_BUCKETS):
        acc = jnp.where(bkt == b, tab_ref[b, h], acc)
    o_ref[...] = acc


def _bias_lookup(rel_table, bkt):
    r, c = bkt.shape
    return pl.pallas_call(
        _bias_kernel, out_shape=jax.ShapeDtypeStruct((N_HEADS, r, c), F32), grid=(N_HEADS,),
        in_specs=[pl.BlockSpec(memory_space=pltpu.SMEM), _resident((r, c))],
        out_specs=pl.BlockSpec((None, r, c), lambda h: (h, 0, 0)),
        compiler_params=_cparams(1), name="rel_bias")(rel_table, bkt)


ATTN_WIDE_GROUP = 4
PAGES_PER_BLOCK = MOBA_BLOCK // PAGE_SIZE


class _GateStream:
    def __init__(self, q_sample, cache_k, page_table, n_steps, flat_step):
        bs, n_pages = page_table.shape
        assert (bs * n_pages) % n_steps == 0
        self.pages_per_step = bs * n_pages // n_steps
        assert n_pages % self.pages_per_step == 0 and self.pages_per_step % PAGES_PER_BLOCK == 0
        self.steps_per_seq = n_pages // self.pages_per_step
        self.n_blk = n_pages // PAGES_PER_BLOCK
        assert self.n_blk >= MOBA_TOPK
        self.flat_step = flat_step
        self.operands = [q_sample] + [cache_k] * self.pages_per_step
        self.prefetch = page_table
        self.out_shape = [jax.ShapeDtypeStruct((bs, N_HEADS, self.n_blk), F32),
                          jax.ShapeDtypeStruct((bs, N_HEADS, LANES), I32)]

    def _seq(self, idx):
        return self.flat_step(*idx) // self.steps_per_seq

    def in_specs(self):
        def page_spec(r):
            def index(*a):
                f = self.flat_step(*a[:-1])
                return (a[-1][f // self.steps_per_seq, (f % self.steps_per_seq) * self.pages_per_step + r],
                        0, 0, 0)
            return pl.BlockSpec((None, PAGE_SIZE, N_HEADS, HEAD_DIM), index)

        return ([pl.BlockSpec((None, N_HEADS, HEAD_DIM), lambda *a: (self._seq(a[:-1]), 0, 0))]
                + [page_spec(r) for r in range(self.pages_per_step)])

    def out_specs(self):
        per_seq = lambda *a: (self._seq(a[:-1]), 0, 0)
        return [pl.BlockSpec((None, N_HEADS, self.n_blk), per_seq),
                pl.BlockSpec((None, N_HEADS, LANES), per_seq)]

    def start_step(self, grid_idx, g_ref):
        s = lax.rem(self.flat_step(*grid_idx), self.steps_per_seq)

        @pl.when(s == 0)
        def _():
            g_ref[...] = jnp.zeros(g_ref.shape, F32)

        return s

    @staticmethod
    def top_blocks(tops):
        top = jnp.concatenate(tops, axis=0)
        return top[:, :, :MOBA_TOPK].reshape(top.shape[0], N_HEADS * MOBA_TOPK)


def _gate_step(s, n_blk, q_ref, pages, g_ref, top_ref):
    blocks_per_step = len(pages) // PAGES_PER_BLOCK
    q = q_ref[...]
    lane = lax.broadcasted_iota(I32, (N_HEADS, n_blk), 1)
    acc = g_ref[...]
    for p in range(blocks_per_step):
        bsum = functools.reduce(jnp.add, [jnp.sum(pages[PAGES_PER_BLOCK * p + r][...], axis=0)
                                          for r in range(PAGES_PER_BLOCK)])
        gn = jnp.sum(q * (bsum * (1.0 / MOBA_BLOCK)), axis=1, keepdims=True)
        acc = jnp.where(lane == s * blocks_per_step + p, gn, acc)
    g_ref[...] = acc

    g = acc
    out_lane = lax.broadcasted_iota(I32, (N_HEADS, LANES), 1)
    top = jnp.zeros((N_HEADS, LANES), I32)
    for t in range(MOBA_TOPK):
        m = jnp.max(g, axis=1, keepdims=True)
        idx = jnp.min(jnp.where(g == m, lane, n_blk), axis=1, keepdims=True)
        top = jnp.where(out_lane == t, idx, top)
        g = jnp.where(lane == idx, -jnp.inf, g)
    top_ref[...] = top


def _attn_kernel(nb, tab_ref, q_ref, k_ref, v_ref, bias_ref, o_ref,
                 kb_s, vt_s, km_s, far_s, m_s, l_s, acc_s):
    h = pl.program_id(1)
    i = pl.program_id(2)
    blk = MOBA_BLOCK

    @pl.when(i == 0)
    def _():
        kb_s[...] = k_ref[...].astype(BF16)
        for j in range(nb):
            rows = slice(j * blk, (j + 1) * blk)
            km_s[j:j + 1, :] = jnp.mean(k_ref[rows, :], axis=0, keepdims=True)
            vt_s[:, rows] = v_ref[rows, :].T.astype(BF16)

    q = q_ref[...]
    qt = q.T.astype(BF16)

    gate = lax.dot_general(km_s[...], q, _NT, precision=lax.Precision.HIGHEST,
                           preferred_element_type=F32)
    sub = lax.broadcasted_iota(I32, (nb, blk), 0)
    gate = jnp.where(sub < i, gate, NEG)
    rank = jnp.zeros((nb, blk), I32)
    for jp in range(nb):
        gj = gate[jp:jp + 1, :]
        rank = rank + jnp.where(gj == gate, jnp.where(jp < sub, 1, 0),
                                jnp.where(gj > gate, 1, 0))
    selneg = jnp.where((rank < MOBA_TOPK) & (sub < i), 0.0, NEG)
    far = selneg + tab_ref[REL_BUCKETS - 1, h]

    def row_of(x, j):
        return jnp.where(j >= 0, jnp.sum(jnp.where(sub == j, x, 0.0), axis=0, keepdims=True), NEG)

    n_first = ATTN_WIDE_GROUP
    n_rest = jnp.maximum(i + 1 - n_first, 0)
    far_s[...] = jnp.where(sub < n_rest, far, NEG)

    def partial_softmax(blocks):
        rows = [pl.multiple_of(j * blk, blk) for j, _ in blocks]
        sts = [jnp.dot(kb_s[pl.ds(r, blk), :], qt, preferred_element_type=F32) * SCALE + extra
               for r, (_, extra) in zip(rows, blocks)]
        ms = [jnp.max(st, axis=0, keepdims=True) for st in sts]
        ps = [jnp.exp(st - m) for st, m in zip(sts, ms)]
        ls = [jnp.sum(p, axis=0, keepdims=True) for p in ps]
        os_ = [jnp.dot(vt_s[:, pl.ds(r, blk)], p.astype(BF16), preferred_element_type=F32)
               for r, p in zip(rows, ps)]
        return list(zip(ms, ls, os_))

    def merge(parts):
        m = functools.reduce(jnp.maximum, [pm for pm, _, _ in parts])
        w = [jnp.exp(pm - m) for pm, _, _ in parts]
        l = functools.reduce(jnp.add, [wi * pl_ for wi, (_, pl_, _) in zip(w, parts)])
        o = functools.reduce(jnp.add, [wi * po for wi, (_, _, po) in zip(w, parts)])
        return m, l, o

    first = [(i, bias_ref[0]), (jnp.maximum(i - 1, 0), bias_ref[1] + row_of(selneg, i - 1))]
    first += [(jnp.maximum(i - back, 0), row_of(far, i - back)) for back in range(2, n_first)]
    m0, l0, o0 = merge(partial_softmax(first))
    m_s[...] = m0
    l_s[...] = l0
    acc_s[...] = o0

    def group(g, carry):
        j0 = g * ATTN_WIDE_GROUP
        parts = partial_softmax([(j0 + u, far_s[pl.ds(j0 + u, 1), :]) for u in range(ATTN_WIDE_GROUP)])
        m, l, o = merge([(m_s[...], l_s[...], acc_s[...])] + parts)
        m_s[...] = m
        l_s[...] = l
        acc_s[...] = o
        return carry

    lax.fori_loop(0, (n_rest + ATTN_WIDE_GROUP - 1) // ATTN_WIDE_GROUP, group, 0)

    o_ref[...] = (acc_s[...] / l_s[...]).T


def _prompt_attention(q, k, v, bias, rel_table, batch, seq):
    nb = seq // MOBA_BLOCK
    blk = MOBA_BLOCK
    qo = lambda b, h, i: (b * nb + i, h)
    kv = lambda b, h, i: (b, h)
    return pl.pallas_call(
        functools.partial(_attn_kernel, nb),
        out_shape=jax.ShapeDtypeStruct((batch * seq, ATTN_WIDTH), F32),
        grid=(batch, N_HEADS, nb),
        in_specs=[pl.BlockSpec(memory_space=pltpu.SMEM),
                  pl.BlockSpec((blk, HEAD_DIM), qo),
                  pl.BlockSpec((seq, HEAD_DIM), kv),
                  pl.BlockSpec((seq, HEAD_DIM), kv),
                  pl.BlockSpec((None, 2, blk, blk), lambda b, h, i: (h, 0, 0, 0))],
        out_specs=pl.BlockSpec((blk, HEAD_DIM), qo),
        scratch_shapes=[pltpu.VMEM((seq, HEAD_DIM), BF16), pltpu.VMEM((HEAD_DIM, seq), BF16),
                        pltpu.VMEM((nb, HEAD_DIM), F32), pltpu.VMEM((nb, blk), F32),
                        pltpu.VMEM((1, blk), F32), pltpu.VMEM((1, blk), F32),
                        pltpu.VMEM((HEAD_DIM, blk), F32)],
        compiler_params=_cparams(3), name="moba_prompt")(rel_table, q, k, v, bias)


PAGES_PER_HEAD = MOBA_TOPK * PAGES_PER_BLOCK
PAGES_PER_SEQ = N_HEADS * PAGES_PER_HEAD


def _sattn_kernel(n_blk, pt_ref, top_ref, tab_ref, q_ref, kn_ref, vn_ref, b_last_ref, ck_hbm, cv_hbm,
                  o_ref, kbuf, vbuf, ksem, vsem):
    b = pl.program_id(0)
    slot = lax.rem(b, 2)

    def page_copies(seq, sl):
        out = []
        for h in range(N_HEADS):
            for t in range(MOBA_TOPK):
                blk = top_ref[seq, h * MOBA_TOPK + t]
                for r in range(PAGES_PER_BLOCK):
                    page = pt_ref[seq, PAGES_PER_BLOCK * blk + r]
                    idx = h * PAGES_PER_HEAD + t * PAGES_PER_BLOCK + r
                    out.append(pltpu.make_async_copy(ck_hbm.at[page, :, h, :], kbuf.at[sl, idx], ksem.at[sl]))
                    out.append(pltpu.make_async_copy(cv_hbm.at[page, :, h, :], vbuf.at[sl, idx], vsem.at[sl]))
        return out

    @pl.when(b == 0)
    def _():
        for cp in page_copies(0, 0):
            cp.start()

    @pl.when(b + 1 < pl.num_programs(0))
    def _():
        for cp in page_copies(b + 1, 1 - slot):
            cp.start()

    for cp in page_copies(b, slot):
        cp.wait()

    q = q_ref[...]
    vn = vn_ref[...]
    s_self = jnp.sum(q * kn_ref[...], axis=1, keepdims=True) * SCALE
    sub = lax.broadcasted_iota(I32, (N_HEADS, HEAD_DIM), 0)
    out = jnp.zeros((N_HEADS, HEAD_DIM), F32)
    n_keys = PAGES_PER_HEAD * PAGE_SIZE
    for h in range(N_HEADS):
        pages = slice(h * PAGES_PER_HEAD, (h + 1) * PAGES_PER_HEAD)
        kh = kbuf[slot, pages].reshape(n_keys, HEAD_DIM).astype(BF16)
        vh = vbuf[slot, pages].reshape(n_keys, HEAD_DIM).astype(BF16)
        q8 = jnp.broadcast_to(q[h:h + 1, :], (SUBLANES, HEAD_DIM)).astype(BF16)
        s = lax.dot_general(q8, kh, _NT, preferred_element_type=F32)[:1, :] * SCALE
        far_bias = tab_ref[REL_BUCKETS - 1, h]
        bias = []
        for t in range(MOBA_TOPK):
            is_last = top_ref[b, h * MOBA_TOPK + t] == n_blk - 1
            bias.append(jnp.where(is_last, b_last_ref[h:h + 1, :], far_bias))
        s = s + jnp.concatenate(bias, axis=1)
        sh = s_self[h:h + 1, :] + tab_ref[0, h]
        m = jnp.maximum(jnp.max(s, axis=1, keepdims=True), sh)
        p = jnp.exp(s - m)
        p_self = jnp.exp(sh - m)
        l = jnp.sum(p, axis=1, keepdims=True) + p_self
        p8 = jnp.broadcast_to(p, (SUBLANES, n_keys)).astype(BF16)
        o = jnp.dot(p8, vh, preferred_element_type=F32)[:1, :] + p_self * vn[h:h + 1, :]
        out = jnp.where(sub == h, o / l, out)
    o_ref[...] = out


def _sample_attention(q3, kn3, vn3, cache_k, cache_v, page_table, top, rel_table, bias_last):
    bs = q3.shape[0]
    n_blk = page_table.shape[1] * PAGE_SIZE // MOBA_BLOCK
    per_b = pl.BlockSpec((None, N_HEADS, HEAD_DIM), lambda b, pt, tp: (b, 0, 0))
    buf = pltpu.VMEM((2, PAGES_PER_SEQ, PAGE_SIZE, HEAD_DIM), F32)
    grid_spec = pltpu.PrefetchScalarGridSpec(
        num_scalar_prefetch=2, grid=(bs,),
        in_specs=[pl.BlockSpec(memory_space=pltpu.SMEM), per_b, per_b, per_b,
                  pl.BlockSpec((N_HEADS, MOBA_BLOCK), lambda b, pt, tp: (0, 0)),
                  pl.BlockSpec(memory_space=pl.ANY), pl.BlockSpec(memory_space=pl.ANY)],
        out_specs=per_b,
        scratch_shapes=[buf, buf, pltpu.SemaphoreType.DMA((2,)), pltpu.SemaphoreType.DMA((2,))])
    out = pl.pallas_call(
        functools.partial(_sattn_kernel, n_blk),
        out_shape=jax.ShapeDtypeStruct((bs, N_HEADS, HEAD_DIM), F32),
        grid_spec=grid_spec, compiler_params=_cparams(1), name="moba_sample")(
            page_table, top, rel_table, q3, kn3, vn3, bias_last, cache_k, cache_v)
    return out.reshape(bs, ATTN_WIDTH)


CONV_HALO = 32
CONV_ROWS = 64


def _ln_swish(y, g, b):
    mu = jnp.mean(y, axis=-1, keepdims=True)
    var = jnp.mean(jnp.square(y - mu), axis=-1, keepdims=True)
    yn = (y - mu) * lax.rsqrt(var + EPS) * g + b
    return yn * jax.nn.sigmoid(yn)


def _conv_kernel(tt, cur_ref, halo_ref, w_ref, b_ref, lg_ref, lb_ref, y_ref, ext_s, yc_s):
    t = pl.program_id(1)
    ch = cur_ref.shape[1]
    ext_s[CONV_HALO:CONV_HALO + tt, :] = cur_ref[...]
    ext_s[CONV_HALO + tt:, :] = jnp.zeros((SUBLANES, ch), F32)

    @pl.when(t == 0)
    def _():
        ext_s[:CONV_HALO, :] = jnp.zeros((CONV_HALO, ch), F32)

    @pl.when(t > 0)
    def _():
        ext_s[:CONV_HALO, :] = halo_ref[...]

    lead = CONV_HALO - (CONV_WIDTH - 1)
    rc = CONV_ROWS

    def chunk(r, carry):
        r0 = pl.multiple_of(r * rc, rc)
        for lt in range(ch // LANES):
            ls = slice(lt * LANES, (lt + 1) * LANES)
            win = ext_s[pl.ds(r0, rc + CONV_HALO + SUBLANES), ls]
            acc = None
            for sh in range(SUBLANES):
                z = None
                for a in range(CONV_HALO // SUBLANES + 1):
                    k = SUBLANES * a + sh - lead
                    if 0 <= k < CONV_WIDTH:
                        term = w_ref[k:k + 1, ls] * win[SUBLANES * a:SUBLANES * a + rc + SUBLANES]
                        z = term if z is None else z + term
                z = z[sh:sh + rc]
                acc = z if acc is None else acc + z
            yc_s[pl.ds(r0, rc), ls] = acc + b_ref[:, ls]
        return carry

    lax.fori_loop(0, tt // rc, chunk, 0)
    y_ref[...] = _ln_swish(yc_s[...], lg_ref[...], lb_ref[...])


def _conv_prompt(u, w_dw, b_dw, ln_g, ln_b, batch, seq):
    ch = u.shape[1]
    tt = min(seq, 512)
    nt = seq // tt
    hpt = tt // CONV_HALO
    vec = lambda x: x.reshape(1, ch)
    return pl.pallas_call(
        functools.partial(_conv_kernel, tt),
        out_shape=jax.ShapeDtypeStruct((batch * seq, ch), F32), grid=(batch, nt),
        in_specs=[pl.BlockSpec((tt, ch), lambda b, t: (b * nt + t, 0)),
                  pl.BlockSpec((CONV_HALO, ch), lambda b, t: (jnp.maximum((b * nt + t) * hpt - 1, 0), 0)),
                  _resident(w_dw.shape), _resident((1, ch)), _resident((1, ch)), _resident((1, ch))],
        out_specs=pl.BlockSpec((tt, ch), lambda b, t: (b * nt + t, 0)),
        scratch_shapes=[pltpu.VMEM((tt + CONV_HALO + SUBLANES, ch), F32), pltpu.VMEM((tt, ch), F32)],
        compiler_params=_cparams(2), name="conv_prompt")(u, u, w_dw, vec(b_dw), vec(ln_g), vec(ln_b))


def _conv_step_kernel(st_ref, u_ref, w_ref, b_ref, lg_ref, lb_ref, y_ref, ns_ref):
    hist = CONV_WIDTH - 1
    u = u_ref[...]
    acc = w_ref[hist:hist + 1, :] * u
    for k in range(hist):
        acc = acc + w_ref[k:k + 1, :] * st_ref[:, k, :]
    y_ref[...] = _ln_swish(acc + b_ref[...], lg_ref[...], lb_ref[...])
    for k in range(hist - 1):
        ns_ref[:, k, :] = st_ref[:, k + 1, :]
    ns_ref[:, hist - 1, :] = u


def _conv_step(state, u, w_dw, b_dw, ln_g, ln_b):
    bs, hist, ch = state.shape
    vec = lambda x: x.reshape(1, ch)
    return pl.pallas_call(
        _conv_step_kernel,
        out_shape=(jax.ShapeDtypeStruct((bs, ch), F32), jax.ShapeDtypeStruct((bs, hist, ch), F32)),
        compiler_params=pltpu.CompilerParams(vmem_limit_bytes=VMEM_LIMIT_BYTES),
        name="conv_step")(state, u, w_dw, vec(b_dw), vec(ln_g), vec(ln_b))


def _mix_kernel(attn_ref, conv_ref, x_ref, wo_ref, g_ref, wq_ref, h_ref, xn_ref, pq_ref):
    mix = jnp.concatenate([attn_ref[...], conv_ref[...]], axis=1).astype(BF16)
    h = x_ref[...] + jnp.dot(mix, wo_ref[...], preferred_element_type=F32)
    h_ref[...] = h
    xn = _rmsnorm(h, g_ref[...]).astype(BF16)
    xn_ref[...] = xn
    pq_ref[...] = jnp.dot(xn, wq_ref[...], preferred_element_type=F32)


def _mix_out(attn, conv, x, wo_bf16, g, wq_bf16):
    m, d = x.shape
    tm = min(m, 256)
    row = lambda i: (i, 0)
    half = attn.shape[1]
    return pl.pallas_call(
        _mix_kernel,
        out_shape=(jax.ShapeDtypeStruct((m, d), F32), jax.ShapeDtypeStruct((m, d), BF16),
                   jax.ShapeDtypeStruct((m, wq_bf16.shape[1]), F32)),
        grid=(m // tm,),
        in_specs=[pl.BlockSpec((tm, half), row), pl.BlockSpec((tm, half), row), pl.BlockSpec((tm, d), row),
                  _resident(wo_bf16.shape), _resident((1, d)), _resident(wq_bf16.shape)],
        out_specs=(pl.BlockSpec((tm, d), row), pl.BlockSpec((tm, d), row),
                   pl.BlockSpec((tm, wq_bf16.shape[1]), row)),
        compiler_params=_cparams(1), name="mix_out")(attn, conv, x, wo_bf16, g.reshape(1, d), wq_bf16)


ROUTE_TOKENS = LANES
_BIG = 1 << 20


def _candidate_layout():
    k = PEER_TOPK
    groups, flat = [], []
    for a in range(k // 2):
        nb = k // (a + 1)
        rows = -(-nb // SUBLANES) * SUBLANES
        groups.append((a, 1, rows))
        flat += [a * k + b if b < nb else _BIG for b in range(rows)]
    groups.append((k // 2, k // 2, 1))
    flat += [a * k for a in range(k // 2, k)]
    return groups, np.asarray(flat, np.int32)


def _route_kernel(pq_ref, sk_ref, flat_ref, e_ref, g_ref, et_s, gt_s):
    k = PEER_TOPK
    nk = PEER_NKEYS
    tn = ROUTE_TOKENS
    row = lax.broadcasted_iota(I32, (nk, tn), 0)
    groups, _ = _candidate_layout()
    flat = flat_ref[...]

    def half_topk(h, c):
        off = pl.multiple_of((h * 2 + c) * nk, nk)
        s = lax.dot_general(sk_ref[h, c], pq_ref[:, pl.ds(off, nk)], _NT,
                            precision=lax.Precision.HIGHEST, preferred_element_type=F32)
        ts, ti = [], []
        for _ in range(k):
            m = jnp.max(s, axis=0, keepdims=True)
            idx = jnp.min(jnp.where(s == m, row, nk), axis=0, keepdims=True)
            ts.append(m)
            ti.append(idx)
            s = jnp.where(row == idx, -jnp.inf, s)
        return jnp.concatenate(ts, axis=0), jnp.concatenate(ti, axis=0)

    def head(h, carry):
        s0, i0 = half_topk(h, 0)
        s1, i1 = half_topk(h, 1)
        cs, ce = [], []
        for a0, na, nb in groups:
            if na == 1:
                cs.append(s0[a0:a0 + 1] + s1[:nb])
                ce.append(i0[a0:a0 + 1] * nk + i1[:nb])
            else:
                cs.append(s0[a0:a0 + na] + s1[:1])
                ce.append(i0[a0:a0 + na] * nk + i1[:1])
        cand = jnp.where(flat < _BIG, jnp.concatenate(cs, axis=0), -jnp.inf)
        cexp = jnp.concatenate(ce, axis=0)
        bs, be = [], []
        for _ in range(k):
            m = jnp.max(cand, axis=0, keepdims=True)
            idx = jnp.min(jnp.where(cand == m, flat, _BIG), axis=0, keepdims=True)
            hit = flat == idx
            bs.append(m)
            be.append(jnp.max(jnp.where(hit, cexp, -1), axis=0, keepdims=True))
            cand = jnp.where(hit, -jnp.inf, cand)
        best = jnp.concatenate(bs, axis=0)
        p = jnp.exp(best - best[:1])
        r = pl.multiple_of(h * k, k)
        gt_s[pl.ds(r, k), :] = p / jnp.sum(p, axis=0, keepdims=True)
        et_s[pl.ds(r, k), :] = jnp.concatenate(be, axis=0)
        return carry

    lax.fori_loop(0, PEER_HEADS, head, 0, unroll=4)
    e_ref[...] = et_s[...].T
    g_ref[...] = gt_s[...].T


def _route(pq, sub_keys):
    n, qd = pq.shape
    tn = ROUTE_TOKENS
    _, flat = _candidate_layout()
    flat = jnp.asarray(np.broadcast_to(flat[:, None], (flat.shape[0], tn)))
    row = lambda i: (i, 0)
    return pl.pallas_call(
        _route_kernel,
        out_shape=(jax.ShapeDtypeStruct((n, PEER_SLOTS), I32), jax.ShapeDtypeStruct((n, PEER_SLOTS), F32)),
        grid=(n // tn,),
        in_specs=[pl.BlockSpec((tn, qd), row), _resident(sub_keys.shape), _resident(flat.shape)],
        out_specs=(pl.BlockSpec((tn, PEER_SLOTS), row), pl.BlockSpec((tn, PEER_SLOTS), row)),
        scratch_shapes=[pltpu.VMEM((PEER_SLOTS, tn), I32), pltpu.VMEM((PEER_SLOTS, tn), F32)],
        compiler_params=_cparams(1), name="peer_route")(pq, sub_keys, flat)


EXPERT_CHUNK = 1024
_NKEYS_BITS = PEER_NKEYS.bit_length() - 1
assert 1 << _NKEYS_BITS == PEER_NKEYS


def _split_expert(e):
    return lax.shift_right_logical(e, _NKEYS_BITS), e & (PEER_NKEYS - 1)


def _host_call(kernel_fn, name, grid, in_specs, out_specs, out_shape, scratch_shapes, operands, gate_args,
               drain_steps=0):
    n_chunks = grid[1] - drain_steps
    stream = None
    if gate_args is not None:
        stream = _GateStream(*gate_args, n_steps=grid[0] * n_chunks,
                             flat_step=lambda i, c: i * n_chunks + jnp.minimum(c, n_chunks - 1))
        in_specs = in_specs + stream.in_specs()
        out_specs = out_specs + stream.out_specs()
        out_shape = out_shape + stream.out_shape
        operands = [stream.prefetch] + operands + stream.operands
    grid_spec = pltpu.PrefetchScalarGridSpec(
        num_scalar_prefetch=0 if stream is None else 1, grid=grid, in_specs=in_specs, out_specs=out_specs,
        scratch_shapes=scratch_shapes)
    return pl.pallas_call(functools.partial(kernel_fn, stream), out_shape=out_shape, grid_spec=grid_spec,
                          compiler_params=_cparams(2, HOST_VMEM_LIMIT_BYTES), name=name)(*operands)


def _split_host_refs(stream, refs, n_in, n_out):
    if stream is None:
        return refs[:n_in], refs[n_in:n_in + n_out], refs[n_in + n_out:], None
    refs = refs[1:]
    n_gate_in = 1 + stream.pages_per_step
    gate_in = refs[n_in:n_in + n_gate_in]
    outs = refs[n_in + n_gate_in:]
    gate = (gate_in[0], gate_in[1:], outs[n_out], outs[n_out + 1])
    return refs[:n_in], outs[:n_out], outs[n_out + 2:], gate


def _peer_up_kernel(stream, *refs):
    (xn_ref, e_ref, u_ref), (hs_ref,), (hd_s,), gate = _split_host_refs(stream, refs, 3, 1)
    c = pl.program_id(1)
    groups = EXPERT_CHUNK // PEER_NKEYS
    if gate is not None:
        gate_s = stream.start_step((pl.program_id(0), c), gate[2])

    @pl.when(c == 0)
    def _():
        hs_ref[...] = jnp.zeros(hs_ref.shape, F32)
        hd_s[1] = jnp.zeros(hd_s.shape[1:], F32)

    def step(cur):
        if gate is not None:
            _gate_step(gate_s, stream.n_blk, *gate)
        xn = xn_ref[...]
        for j in range(EXPERT_CHUNK // MXU_WIDTH):
            hd_s[cur, :, j * MXU_WIDTH:(j + 1) * MXU_WIDTH] = lax.dot_general(
                xn, u_ref[j * MXU_WIDTH:(j + 1) * MXU_WIDTH, :].astype(BF16), _NT, preferred_element_type=F32)
        e = e_ref[...]
        i1, i2 = _split_expert(e)
        acc = hs_ref[...]
        for g in range(groups):
            got = jnp.take_along_axis(hd_s[1 - cur, :, g * PEER_NKEYS:(g + 1) * PEER_NKEYS], i2, axis=1)
            acc = jnp.where(i1 == (c - 1) * groups + g, got, acc)
        hs_ref[...] = acc

    for parity in range(2):
        pl.when(lax.rem(c, 2) == parity)(functools.partial(step, parity))


def _peer_up(xn_bf16, e, u_tab, gate_args=None):
    n, d = xn_bf16.shape
    n_exp = u_tab.shape[0]
    tn = 1024 if n % 1024 == 0 else (256 if n % 256 == 0 else n)
    n_chunks = n_exp // EXPERT_CHUNK
    tok = lambda i, c, *_: (i, 0)
    return _host_call(
        _peer_up_kernel, "peer_up", (n // tn, n_chunks + 1),
        [pl.BlockSpec((tn, d), tok), pl.BlockSpec((tn, PEER_SLOTS), tok),
         pl.BlockSpec((EXPERT_CHUNK, d), lambda i, c, *_: (jnp.minimum(c, n_chunks - 1), 0))],
        [pl.BlockSpec((tn, PEER_SLOTS), tok)], [jax.ShapeDtypeStruct((n, PEER_SLOTS), F32)],
        [pltpu.VMEM((2, tn, EXPERT_CHUNK), F32)], [xn_bf16, e, u_tab], gate_args, drain_steps=1)


TOKEN_BATCH = 2 * SUBLANES
DOWN_EXPERT_CHUNK = 1024


def _peer_down_kernel(tn, stream, *refs):
    (e_ref, gate_ref, hs_ref, v_ref, h_ref, gf_ref), (y_ref,), (act_s, a_s), block_gate = _split_host_refs(
        stream, refs, 6, 1)
    c = pl.program_id(1)
    nk = PEER_NKEYS
    groups = DOWN_EXPERT_CHUNK // nk
    if block_gate is not None:
        gate_s = stream.start_step((pl.program_id(0), c), block_gate[2])

    @pl.when(c == 0)
    def _():
        y_ref[...] = jnp.zeros(y_ref.shape, F32)
        hv = hs_ref[...]
        gelu = 0.5 * hv * (1.0 + lax.erf(hv * math.sqrt(0.5)))
        act_s[...] = gate_ref[...] * gelu
        sub = lax.broadcasted_iota(I32, (nk, PEER_SLOTS), 0)

        def scatter_token(n):
            e = e_ref[pl.ds(n, 1), :]
            a = act_s[pl.ds(n, 1), :]
            i1, i2 = _split_expert(e)
            pt = jnp.where(sub == i1, a, 0.0).astype(BF16)
            qt = jnp.where(sub == i2, 1.0, 0.0).astype(BF16)
            return lax.dot_general(pt, qt, _NT, preferred_element_type=F32)

        def batch(t, carry):
            n0 = pl.multiple_of(t * TOKEN_BATCH, TOKEN_BATCH)
            halves = []
            for s0 in range(0, TOKEN_BATCH, SUBLANES):
                per_token = jnp.stack([scatter_token(n0 + s0 + k) for k in range(SUBLANES)], axis=0)
                halves.append(jnp.swapaxes(per_token, 0, 1))
            a_s[:, pl.ds(n0, TOKEN_BATCH), :] = jnp.concatenate(halves, axis=1).astype(BF16)
            return carry

        lax.fori_loop(0, tn // TOKEN_BATCH, batch, 0)

    if block_gate is not None:
        _gate_step(gate_s, stream.n_blk, *block_gate)

    lhs = jnp.concatenate([a_s[c * groups + g] for g in range(groups)], axis=1)
    y_ref[...] += jnp.dot(lhs, v_ref[...], preferred_element_type=F32)

    @pl.when(c == pl.num_programs(1) - 1)
    def _():
        y_ref[...] = _rmsnorm(h_ref[...] + y_ref[...], gf_ref[...])


def _peer_down(e, gate, hs, v_bf16, h, g_final, gate_args=None):
    n, d = h.shape
    n_exp = v_bf16.shape[0]
    tn = 512 if n % 512 == 0 else n
    tok = lambda i, c, *_: (i, 0)
    return _host_call(
        functools.partial(_peer_down_kernel, tn), "peer_down", (n // tn, n_exp // DOWN_EXPERT_CHUNK),
        [pl.BlockSpec((tn, PEER_SLOTS), tok), pl.BlockSpec((tn, PEER_SLOTS), tok),
         pl.BlockSpec((tn, PEER_SLOTS), tok),
         pl.BlockSpec((DOWN_EXPERT_CHUNK, d), lambda i, c, *_: (c, 0)),
         pl.BlockSpec((tn, d), tok, pipeline_mode=pl.Buffered(1)),
         pl.BlockSpec((1, d), lambda i, c, *_: (0, 0), pipeline_mode=pl.Buffered(1))],
        [pl.BlockSpec((tn, d), tok)], [jax.ShapeDtypeStruct((n, d), F32)],
        [pltpu.VMEM((tn, PEER_SLOTS), F32), pltpu.VMEM((PEER_NKEYS, tn, PEER_NKEYS), BF16)],
        [e, gate, hs, v_bf16, h, g_final.reshape(1, d)], gate_args)


def _peer_and_final_norm(h, xn_bf16, pq, sub_keys, u_tab, v_bf16, g_final, gate_args=None):
    n = h.shape[0]
    n_pad = -(-n // ROUTE_TOKENS) * ROUTE_TOKENS
    pq_pad = pq if n_pad == n else jnp.pad(pq, ((0, n_pad - n), (0, 0)))
    e, gate = _route(pq_pad, sub_keys)
    e, gate = e[:n], gate[:n]
    if gate_args is None:
        (hs,) = _peer_up(xn_bf16, e, u_tab)
        (y,) = _peer_down(e, gate, hs, v_bf16, h, g_final)
        return y, None
    q_sample, cache_k, page_table = gate_args
    half = q_sample.shape[0] // 2
    hs, _, top_a = _peer_up(xn_bf16, e, u_tab, (q_sample[:half], cache_k, page_table[:half]))
    y, _, top_b = _peer_down(e, gate, hs, v_bf16, h, g_final, (q_sample[half:], cache_k, page_table[half:]))
    return y, _GateStream.top_blocks([top_a, top_b])


def kernel(x_prompt, x_sample, cache_k, cache_v, state_conv, page_table, rel_bias_table, norm_mix_g, w_in,
           w_dw, b_dw, conv_ln_g, conv_ln_b, w_out, norm_ffn_g, peer_wq, peer_sub_keys, peer_u, peer_v,
           final_norm_g):
    batch, seq, d = x_prompt.shape
    bs, s_new, _ = x_sample.shape
    depth = w_in.shape[0]
    assert s_new == 1 and depth == 1 and seq % MOBA_BLOCK == 0 and seq // MOBA_BLOCK >= MOBA_TOPK
    hist = CONV_WIDTH - 1
    blk = MOBA_BLOCK

    dist = np.arange(blk)[None, :] - np.arange(blk)[:, None]
    bkt_prompt = jnp.concatenate([_rel_bucket(dist), _rel_bucket(dist + blk)], axis=0)
    bias_prompt = _bias_lookup(rel_bias_table, bkt_prompt).reshape(N_HEADS, 2, blk, blk)
    bkt_last = jnp.broadcast_to(_rel_bucket(blk - np.arange(blk))[None, :], (SUBLANES, blk))
    bias_last = _bias_lookup(rel_bias_table, bkt_last)[:, 0, :]

    hp = x_prompt.reshape(batch * seq, d)
    hs_ = x_sample.reshape(bs, d)
    l = 0
    w_in_b = w_in[l].astype(BF16)
    w_out_b = w_out[l].astype(BF16)
    wq_b = peer_wq[l].astype(BF16)
    u_b = peer_u[l]
    v_b = peer_v[l].astype(BF16)

    qp, kp, vp, up = _proj(hp, norm_mix_g[l], w_in_b)
    qs, ks, vs, us = _proj(hs_, norm_mix_g[l], w_in_b)
    q3 = qs.reshape(bs, N_HEADS, HEAD_DIM)

    attn_p = _prompt_attention(qp, kp, vp, bias_prompt, rel_bias_table, batch, seq)
    conv_p = _conv_prompt(up, w_dw[l], b_dw[l], conv_ln_g[l], conv_ln_b[l], batch, seq)
    h_p, xn_p, pq_p = _mix_out(attn_p, conv_p, hp, w_out_b, norm_ffn_g[l], wq_b)
    y_p, top = _peer_and_final_norm(h_p, xn_p, pq_p, peer_sub_keys[l], u_b, v_b, final_norm_g,
                                    (q3, cache_k[l], page_table))

    attn_s = _sample_attention(q3, ks.reshape(q3.shape), vs.reshape(q3.shape), cache_k[l], cache_v[l],
                               page_table, top, rel_bias_table, bias_last)
    conv_s, state_new = _conv_step(state_conv[l], us, w_dw[l], b_dw[l], conv_ln_g[l], conv_ln_b[l])
    h_s, xn_s, pq_s = _mix_out(attn_s, conv_s, hs_, w_out_b, norm_ffn_g[l], wq_b)
    y_s, _ = _peer_and_final_norm(h_s, xn_s, pq_s, peer_sub_keys[l], u_b, v_b, final_norm_g)

    kv_p = (1, batch, seq, N_HEADS, HEAD_DIM)
    kv_s = (1, bs, 1, N_HEADS, HEAD_DIM)
    conv_prompt_new = up.reshape(batch, seq, -1)[:, seq - hist:, :][None]
    return (y_p.reshape(batch, seq, d), y_s.reshape(bs, 1, d),
            kp.reshape(kv_p), vp.reshape(kv_p), conv_prompt_new,
            ks.reshape(kv_s), vs.reshape(kv_s), state_new[None])
```

```python
import functools
import math

import numpy as np
import jax
import jax.numpy as jnp
from jax import lax
from jax.experimental import pallas as pl
from jax.experimental.pallas import tpu as pltpu

F32, BF16, I32 = jnp.float32, jnp.bfloat16, jnp.int32

N_HEADS = 8
HEAD_DIM = 128
ATTN_WIDTH = N_HEADS * HEAD_DIM
MOBA_BLOCK = 256
MOBA_TOPK = 3
PAGE_SIZE = 128
REL_BUCKETS = 32
REL_MAX_DIST = 128
CONV_WIDTH = 31
PEER_HEADS = 8
PEER_NKEYS = 128
PEER_TOPK = 16
PEER_SLOTS = PEER_HEADS * PEER_TOPK
EPS = 1e-6
NEG = -1e30
SCALE = HEAD_DIM ** -0.5

LANES = 128
SUBLANES = 8
MXU_WIDTH = 256
VMEM_LIMIT_BYTES = 56 * 1024 * 1024
HOST_VMEM_LIMIT_BYTES = 60 * 1024 * 1024

_NT = (((1,), (1,)), ((), ()))


def _cparams(n_axes, vmem_limit_bytes=VMEM_LIMIT_BYTES):
    return pltpu.CompilerParams(dimension_semantics=("arbitrary",) * n_axes,
                                vmem_limit_bytes=vmem_limit_bytes)


def _resident(shape):
    nd = len(shape)
    return pl.BlockSpec(shape, lambda *_: (0,) * nd, pipeline_mode=pl.Buffered(1))


def _rmsnorm(x, g):
    return x * lax.rsqrt(jnp.mean(x * x, axis=-1, keepdims=True) + EPS) * g


def _proj_kernel(x_ref, g_ref, w_ref, q_ref, k_ref, v_ref, u_ref):
    xn = _rmsnorm(x_ref[...], g_ref[...]).astype(BF16)
    c = ATTN_WIDTH

    def mm(j):
        return jnp.dot(xn, w_ref[:, j * c:(j + 1) * c], preferred_element_type=F32)

    q_ref[...] = mm(0)
    k_ref[...] = mm(1)
    v_ref[...] = mm(2)
    u_ref[...] = mm(3) * jax.nn.sigmoid(mm(4))


def _proj(x, g, w_bf16):
    m, d = x.shape
    tm = min(m, 256)
    row = lambda i: (i, 0)
    out = jax.ShapeDtypeStruct((m, ATTN_WIDTH), F32)
    return pl.pallas_call(
        _proj_kernel, out_shape=(out,) * 4, grid=(m // tm,),
        in_specs=[pl.BlockSpec((tm, d), row), _resident((1, d)), _resident(w_bf16.shape)],
        out_specs=(pl.BlockSpec((tm, ATTN_WIDTH), row),) * 4,
        compiler_params=_cparams(1), name="proj")(x, g.reshape(1, d), w_bf16)


MASKED_BUCKET = REL_BUCKETS


def _rel_bucket(dist):
    dist = jnp.asarray(dist, I32)
    n = jnp.maximum(dist, 0)
    max_exact = REL_BUCKETS // 2
    nf = jnp.maximum(n, 1).astype(F32)
    large = max_exact + (jnp.log(nf / max_exact) / math.log(REL_MAX_DIST / max_exact)
                         * (REL_BUCKETS - max_exact)).astype(I32)
    bucket = jnp.where(n < max_exact, n, jnp.minimum(large, REL_BUCKETS - 1))
    return jnp.where(dist < 0, MASKED_BUCKET, bucket)


def _bias_kernel(tab_ref, bkt_ref, o_ref):
    h = pl.program_id(0)
    bkt = bkt_ref[...]
    acc = jnp.full(bkt.shape, NEG, F32)
    for b in range(REL_BUCKETS):
        acc = jnp.where(bkt == b, tab_ref[b, h], acc)
    o_ref[...] = acc


def _bias_lookup(rel_table, bkt):
    r, c = bkt.shape
    return pl.pallas_call(
        _bias_kernel, out_shape=jax.ShapeDtypeStruct((N_HEADS, r, c), F32), grid=(N_HEADS,),
        in_specs=[pl.BlockSpec(memory_space=pltpu.SMEM), _resident((r, c))],
        out_specs=pl.BlockSpec((None, r, c), lambda h: (h, 0, 0)),
        compiler_params=_cparams(1), name="rel_bias")(rel_table, bkt)


ATTN_WIDE_GROUP = 4
PAGES_PER_BLOCK = MOBA_BLOCK // PAGE_SIZE


class _GateStream:
    def __init__(self, q_sample, cache_k, page_table, n_steps, flat_step):
        bs, n_pages = page_table.shape
        assert (bs * n_pages) % n_steps == 0
        self.pages_per_step = bs * n_pages // n_steps
        assert n_pages % self.pages_per_step == 0 and self.pages_per_step % PAGES_PER_BLOCK == 0
        self.steps_per_seq = n_pages // self.pages_per_step
        self.n_blk = n_pages // PAGES_PER_BLOCK
        assert self.n_blk >= MOBA_TOPK
        self.flat_step = flat_step
        self.operands = [q_sample] + [cache_k] * self.pages_per_step
        self.prefetch = page_table
        self.out_shape = [jax.ShapeDtypeStruct((bs, N_HEADS, self.n_blk), F32),
                          jax.ShapeDtypeStruct((bs, N_HEADS, LANES), I32)]

    def _seq(self, idx):
        return self.flat_step(*idx) // self.steps_per_seq

    def in_specs(self):
        def page_spec(r):
            def index(*a):
                f = self.flat_step(*a[:-1])
                return (a[-1][f // self.steps_per_seq, (f % self.steps_per_seq) * self.pages_per_step + r],
                        0, 0, 0)
            return pl.BlockSpec((None, PAGE_SIZE, N_HEADS, HEAD_DIM), index)

        return ([pl.BlockSpec((None, N_HEADS, HEAD_DIM), lambda *a: (self._seq(a[:-1]), 0, 0))]
                + [page_spec(r) for r in range(self.pages_per_step)])

    def out_specs(self):
        per_seq = lambda *a: (self._seq(a[:-1]), 0, 0)
        return [pl.BlockSpec((None, N_HEADS, self.n_blk), per_seq),
                pl.BlockSpec((None, N_HEADS, LANES), per_seq)]

    def start_step(self, grid_idx, g_ref):
        s = lax.rem(self.flat_step(*grid_idx), self.steps_per_seq)

        @pl.when(s == 0)
        def _():
            g_ref[...] = jnp.zeros(g_ref.shape, F32)

        return s

    @staticmethod
    def top_blocks(tops):
        top = jnp.concatenate(tops, axis=0)
        return top[:, :, :MOBA_TOPK].reshape(top.shape[0], N_HEADS * MOBA_TOPK)


def _gate_step(s, n_blk, q_ref, pages, g_ref, top_ref):
    blocks_per_step = len(pages) // PAGES_PER_BLOCK
    q = q_ref[...]
    lane = lax.broadcasted_iota(I32, (N_HEADS, n_blk), 1)
    acc = g_ref[...]
    for p in range(blocks_per_step):
        bsum = functools.reduce(jnp.add, [jnp.sum(pages[PAGES_PER_BLOCK * p + r][...], axis=0)
                                          for r in range(PAGES_PER_BLOCK)])
        gn = jnp.sum(q * (bsum * (1.0 / MOBA_BLOCK)), axis=1, keepdims=True)
        acc = jnp.where(lane == s * blocks_per_step + p, gn, acc)
    g_ref[...] = acc

    g = acc
    out_lane = lax.broadcasted_iota(I32, (N_HEADS, LANES), 1)
    top = jnp.zeros((N_HEADS, LANES), I32)
    for t in range(MOBA_TOPK):
        m = jnp.max(g, axis=1, keepdims=True)
        idx = jnp.min(jnp.where(g == m, lane, n_blk), axis=1, keepdims=True)
        top = jnp.where(out_lane == t, idx, top)
        g = jnp.where(lane == idx, -jnp.inf, g)
    top_ref[...] = top


def _attn_kernel(nb, tab_ref, q_ref, k_ref, v_ref, bias_ref, o_ref,
                 kb_s, vt_s, km_s, far_s, m_s, l_s, acc_s):
    h = pl.program_id(1)
    i = pl.program_id(2)
    blk = MOBA_BLOCK

    @pl.when(i == 0)
    def _():
        kb_s[...] = k_ref[...].astype(BF16)
        for j in range(nb):
            rows = slice(j * blk, (j + 1) * blk)
            km_s[j:j + 1, :] = jnp.mean(k_ref[rows, :], axis=0, keepdims=True)
            vt_s[:, rows] = v_ref[rows, :].T.astype(BF16)

    q = q_ref[...]
    qt = q.T.astype(BF16)

    gate = lax.dot_general(km_s[...], q, _NT, precision=lax.Precision.HIGHEST,
                           preferred_element_type=F32)
    sub = lax.broadcasted_iota(I32, (nb, blk), 0)
    gate = jnp.where(sub < i, gate, NEG)
    rank = jnp.zeros((nb, blk), I32)
    for jp in range(nb):
        gj = gate[jp:jp + 1, :]
        rank = rank + jnp.where(gj == gate, jnp.where(jp < sub, 1, 0),
                                jnp.where(gj > gate, 1, 0))
    selneg = jnp.where((rank < MOBA_TOPK) & (sub < i), 0.0, NEG)
    far = selneg + tab_ref[REL_BUCKETS - 1, h]

    def row_of(x, j):
        return jnp.where(j >= 0, jnp.sum(jnp.where(sub == j, x, 0.0), axis=0, keepdims=True), NEG)

    n_first = ATTN_WIDE_GROUP
    n_rest = jnp.maximum(i + 1 - n_first, 0)
    far_s[...] = jnp.where(sub < n_rest, far, NEG)

    def partial_softmax(blocks):
        rows = [pl.multiple_of(j * blk, blk) for j, _ in blocks]
        sts = [jnp.dot(kb_s[pl.ds(r, blk), :], qt, preferred_element_type=F32) * SCALE + extra
               for r, (_, extra) in zip(rows, blocks)]
        ms = [jnp.max(st, axis=0, keepdims=True) for st in sts]
        ps = [jnp.exp(st - m) for st, m in zip(sts, ms)]
        ls = [jnp.sum(p, axis=0, keepdims=True) for p in ps]
        os_ = [jnp.dot(vt_s[:, pl.ds(r, blk)], p.astype(BF16), preferred_element_type=F32)
               for r, p in zip(rows, ps)]
        return list(zip(ms, ls, os_))

    def merge(parts):
        m = functools.reduce(jnp.maximum, [pm for pm, _, _ in parts])
        w = [jnp.exp(pm - m) for pm, _, _ in parts]
        l = functools.reduce(jnp.add, [wi * pl_ for wi, (_, pl_, _) in zip(w, parts)])
        o = functools.reduce(jnp.add, [wi * po for wi, (_, _, po) in zip(w, parts)])
        return m, l, o

    first = [(i, bias_ref[0]), (jnp.maximum(i - 1, 0), bias_ref[1] + row_of(selneg, i - 1))]
    first += [(jnp.maximum(i - back, 0), row_of(far, i - back)) for back in range(2, n_first)]
    m0, l0, o0 = merge(partial_softmax(first))
    m_s[...] = m0
    l_s[...] = l0
    acc_s[...] = o0

    def group(g, carry):
        j0 = g * ATTN_WIDE_GROUP
        parts = partial_softmax([(j0 + u, far_s[pl.ds(j0 + u, 1), :]) for u in range(ATTN_WIDE_GROUP)])
        m, l, o = merge([(m_s[...], l_s[...], acc_s[...])] + parts)
        m_s[...] = m
        l_s[...] = l
        acc_s[...] = o
        return carry

    lax.fori_loop(0, (n_rest + ATTN_WIDE_GROUP - 1) // ATTN_WIDE_GROUP, group, 0)

    o_ref[...] = (acc_s[...] / l_s[...]).T


def _prompt_attention(q, k, v, bias, rel_table, batch, seq):
    nb = seq // MOBA_BLOCK
    blk = MOBA_BLOCK
    qo = lambda b, h, i: (b * nb + i, h)
    kv = lambda b, h, i: (b, h)
    return pl.pallas_call(
        functools.partial(_attn_kernel, nb),
        out_shape=jax.ShapeDtypeStruct((batch * seq, ATTN_WIDTH), F32),
        grid=(batch, N_HEADS, nb),
        in_specs=[pl.BlockSpec(memory_space=pltpu.SMEM),
                  pl.BlockSpec((blk, HEAD_DIM), qo),
                  pl.BlockSpec((seq, HEAD_DIM), kv),
                  pl.BlockSpec((seq, HEAD_DIM), kv),
                  pl.BlockSpec((None, 2, blk, blk), lambda b, h, i: (h, 0, 0, 0))],
        out_specs=pl.BlockSpec((blk, HEAD_DIM), qo),
        scratch_shapes=[pltpu.VMEM((seq, HEAD_DIM), BF16), pltpu.VMEM((HEAD_DIM, seq), BF16),
                        pltpu.VMEM((nb, HEAD_DIM), F32), pltpu.VMEM((nb, blk), F32),
                        pltpu.VMEM((1, blk), F32), pltpu.VMEM((1, blk), F32),
                        pltpu.VMEM((HEAD_DIM, blk), F32)],
        compiler_params=_cparams(3), name="moba_prompt")(rel_table, q, k, v, bias)


PAGES_PER_HEAD = MOBA_TOPK * PAGES_PER_BLOCK
PAGES_PER_SEQ = N_HEADS * PAGES_PER_HEAD


def _sattn_kernel(n_blk, pt_ref, top_ref, tab_ref, q_ref, kn_ref, vn_ref, b_last_ref, ck_hbm, cv_hbm,
                  o_ref, kbuf, vbuf, ksem, vsem):
    b = pl.program_id(0)
    slot = lax.rem(b, 2)

    def page_copies(seq, sl):
        out = []
        for h in range(N_HEADS):
            for t in range(MOBA_TOPK):
                blk = top_ref[seq, h * MOBA_TOPK + t]
                for r in range(PAGES_PER_BLOCK):
                    page = pt_ref[seq, PAGES_PER_BLOCK * blk + r]
                    idx = h * PAGES_PER_HEAD + t * PAGES_PER_BLOCK + r
                    out.append(pltpu.make_async_copy(ck_hbm.at[page, :, h, :], kbuf.at[sl, idx], ksem.at[sl]))
                    out.append(pltpu.make_async_copy(cv_hbm.at[page, :, h, :], vbuf.at[sl, idx], vsem.at[sl]))
        return out

    @pl.when(b == 0)
    def _():
        for cp in page_copies(0, 0):
            cp.start()

    @pl.when(b + 1 < pl.num_programs(0))
    def _():
        for cp in page_copies(b + 1, 1 - slot):
            cp.start()

    for cp in page_copies(b, slot):
        cp.wait()

    q = q_ref[...]
    vn = vn_ref[...]
    s_self = jnp.sum(q * kn_ref[...], axis=1, keepdims=True) * SCALE
    sub = lax.broadcasted_iota(I32, (N_HEADS, HEAD_DIM), 0)
    out = jnp.zeros((N_HEADS, HEAD_DIM), F32)
    n_keys = PAGES_PER_HEAD * PAGE_SIZE
    for h in range(N_HEADS):
        pages = slice(h * PAGES_PER_HEAD, (h + 1) * PAGES_PER_HEAD)
        kh = kbuf[slot, pages].reshape(n_keys, HEAD_DIM).astype(BF16)
        vh = vbuf[slot, pages].reshape(n_keys, HEAD_DIM).astype(BF16)
        q8 = jnp.broadcast_to(q[h:h + 1, :], (SUBLANES, HEAD_DIM)).astype(BF16)
        s = lax.dot_general(q8, kh, _NT, preferred_element_type=F32)[:1, :] * SCALE
        far_bias = tab_ref[REL_BUCKETS - 1, h]
        bias = []
        for t in range(MOBA_TOPK):
            is_last = top_ref[b, h * MOBA_TOPK + t] == n_blk - 1
            bias.append(jnp.where(is_last, b_last_ref[h:h + 1, :], far_bias))
        s = s + jnp.concatenate(bias, axis=1)
        sh = s_self[h:h + 1, :] + tab_ref[0, h]
        m = jnp.maximum(jnp.max(s, axis=1, keepdims=True), sh)
        p = jnp.exp(s - m)
        p_self = jnp.exp(sh - m)
        l = jnp.sum(p, axis=1, keepdims=True) + p_self
        p8 = jnp.broadcast_to(p, (SUBLANES, n_keys)).astype(BF16)
        o = jnp.dot(p8, vh, preferred_element_type=F32)[:1, :] + p_self * vn[h:h + 1, :]
        out = jnp.where(sub == h, o / l, out)
    o_ref[...] = out


def _sample_attention(q3, kn3, vn3, cache_k, cache_v, page_table, top, rel_table, bias_last):
    bs = q3.shape[0]
    n_blk = page_table.shape[1] * PAGE_SIZE // MOBA_BLOCK
    per_b = pl.BlockSpec((None, N_HEADS, HEAD_DIM), lambda b, pt, tp: (b, 0, 0))
    buf = pltpu.VMEM((2, PAGES_PER_SEQ, PAGE_SIZE, HEAD_DIM), F32)
    grid_spec = pltpu.PrefetchScalarGridSpec(
        num_scalar_prefetch=2, grid=(bs,),
        in_specs=[pl.BlockSpec(memory_space=pltpu.SMEM), per_b, per_b, per_b,
                  pl.BlockSpec((N_HEADS, MOBA_BLOCK), lambda b, pt, tp: (0, 0)),
                  pl.BlockSpec(memory_space=pl.ANY), pl.BlockSpec(memory_space=pl.ANY)],
        out_specs=per_b,
        scratch_shapes=[buf, buf, pltpu.SemaphoreType.DMA((2,)), pltpu.SemaphoreType.DMA((2,))])
    out = pl.pallas_call(
        functools.partial(_sattn_kernel, n_blk),
        out_shape=jax.ShapeDtypeStruct((bs, N_HEADS, HEAD_DIM), F32),
        grid_spec=grid_spec, compiler_params=_cparams(1), name="moba_sample")(
            page_table, top, rel_table, q3, kn3, vn3, bias_last, cache_k, cache_v)
    return out.reshape(bs, ATTN_WIDTH)


CONV_HALO = 32
CONV_ROWS = 64


def _ln_swish(y, g, b):
    mu = jnp.mean(y, axis=-1, keepdims=True)
    var = jnp.mean(jnp.square(y - mu), axis=-1, keepdims=True)
    yn = (y - mu) * lax.rsqrt(var + EPS) * g + b
    return yn * jax.nn.sigmoid(yn)


def _conv_kernel(tt, cur_ref, halo_ref, w_ref, b_ref, lg_ref, lb_ref, y_ref, ext_s, yc_s):
    t = pl.program_id(1)
    ch = cur_ref.shape[1]
    ext_s[CONV_HALO:CONV_HALO + tt, :] = cur_ref[...]
    ext_s[CONV_HALO + tt:, :] = jnp.zeros((SUBLANES, ch), F32)

    @pl.when(t == 0)
    def _():
        ext_s[:CONV_HALO, :] = jnp.zeros((CONV_HALO, ch), F32)

    @pl.when(t > 0)
    def _():
        ext_s[:CONV_HALO, :] = halo_ref[...]

    lead = CONV_HALO - (CONV_WIDTH - 1)
    rc = CONV_ROWS

    def chunk(r, carry):
        r0 = pl.multiple_of(r * rc, rc)
        for lt in range(ch // LANES):
            ls = slice(lt * LANES, (lt + 1) * LANES)
            win = ext_s[pl.ds(r0, rc + CONV_HALO + SUBLANES), ls]
            acc = None
            for sh in range(SUBLANES):
                z = None
                for a in range(CONV_HALO // SUBLANES + 1):
                    k = SUBLANES * a + sh - lead
                    if 0 <= k < CONV_WIDTH:
                        term = w_ref[k:k + 1, ls] * win[SUBLANES * a:SUBLANES * a + rc + SUBLANES]
                        z = term if z is None else z + term
                z = z[sh:sh + rc]
                acc = z if acc is None else acc + z
            yc_s[pl.ds(r0, rc), ls] = acc + b_ref[:, ls]
        return carry

    lax.fori_loop(0, tt // rc, chunk, 0)
    y_ref[...] = _ln_swish(yc_s[...], lg_ref[...], lb_ref[...])


def _conv_prompt(u, w_dw, b_dw, ln_g, ln_b, batch, seq):
    ch = u.shape[1]
    tt = min(seq, 512)
    nt = seq // tt
    hpt = tt // CONV_HALO
    vec = lambda x: x.reshape(1, ch)
    return pl.pallas_call(
        functools.partial(_conv_kernel, tt),
        out_shape=jax.ShapeDtypeStruct((batch * seq, ch), F32), grid=(batch, nt),
        in_specs=[pl.BlockSpec((tt, ch), lambda b, t: (b * nt + t, 0)),
                  pl.BlockSpec((CONV_HALO, ch), lambda b, t: (jnp.maximum((b * nt + t) * hpt - 1, 0), 0)),
                  _resident(w_dw.shape), _resident((1, ch)), _resident((1, ch)), _resident((1, ch))],
        out_specs=pl.BlockSpec((tt, ch), lambda b, t: (b * nt + t, 0)),
        scratch_shapes=[pltpu.VMEM((tt + CONV_HALO + SUBLANES, ch), F32), pltpu.VMEM((tt, ch), F32)],
        compiler_params=_cparams(2), name="conv_prompt")(u, u, w_dw, vec(b_dw), vec(ln_g), vec(ln_b))


def _conv_step_kernel(st_ref, u_ref, w_ref, b_ref, lg_ref, lb_ref, y_ref, ns_ref):
    hist = CONV_WIDTH - 1
    u = u_ref[...]
    acc = w_ref[hist:hist + 1, :] * u
    for k in range(hist):
        acc = acc + w_ref[k:k + 1, :] * st_ref[:, k, :]
    y_ref[...] = _ln_swish(acc + b_ref[...], lg_ref[...], lb_ref[...])
    for k in range(hist - 1):
        ns_ref[:, k, :] = st_ref[:, k + 1, :]
    ns_ref[:, hist - 1, :] = u


def _conv_step(state, u, w_dw, b_dw, ln_g, ln_b):
    bs, hist, ch = state.shape
    vec = lambda x: x.reshape(1, ch)
    return pl.pallas_call(
        _conv_step_kernel,
        out_shape=(jax.ShapeDtypeStruct((bs, ch), F32), jax.ShapeDtypeStruct((bs, hist, ch), F32)),
        compiler_params=pltpu.CompilerParams(vmem_limit_bytes=VMEM_LIMIT_BYTES),
        name="conv_step")(state, u, w_dw, vec(b_dw), vec(ln_g), vec(ln_b))


def _mix_kernel(attn_ref, conv_ref, x_ref, wo_ref, g_ref, wq_ref, h_ref, xn_ref, pq_ref):
    mix = jnp.concatenate([attn_ref[...], conv_ref[...]], axis=1).astype(BF16)
    h = x_ref[...] + jnp.dot(mix, wo_ref[...], preferred_element_type=F32)
    h_ref[...] = h
    xn = _rmsnorm(h, g_ref[...]).astype(BF16)
    xn_ref[...] = xn
    pq_ref[...] = jnp.dot(xn, wq_ref[...], preferred_element_type=F32)


def _mix_out(attn, conv, x, wo_bf16, g, wq_bf16):
    m, d = x.shape
    tm = min(m, 256)
    row = lambda i: (i, 0)
    half = attn.shape[1]
    return pl.pallas_call(
        _mix_kernel,
        out_shape=(jax.ShapeDtypeStruct((m, d), F32), jax.ShapeDtypeStruct((m, d), BF16),
                   jax.ShapeDtypeStruct((m, wq_bf16.shape[1]), F32)),
        grid=(m // tm,),
        in_specs=[pl.BlockSpec((tm, half), row), pl.BlockSpec((tm, half), row), pl.BlockSpec((tm, d), row),
                  _resident(wo_bf16.shape), _resident((1, d)), _resident(wq_bf16.shape)],
        out_specs=(pl.BlockSpec((tm, d), row), pl.BlockSpec((tm, d), row),
                   pl.BlockSpec((tm, wq_bf16.shape[1]), row)),
        compiler_params=_cparams(1), name="mix_out")(attn, conv, x, wo_bf16, g.reshape(1, d), wq_bf16)


ROUTE_TOKENS = LANES
_BIG = 1 << 20


def _candidate_layout():
    k = PEER_TOPK
    groups, flat = [], []
    for a in range(k // 2):
        nb = k // (a + 1)
        rows = -(-nb // SUBLANES) * SUBLANES
        groups.append((a, 1, rows))
        flat += [a * k + b if b < nb else _BIG for b in range(rows)]
    groups.append((k // 2, k // 2, 1))
    flat += [a * k for a in range(k // 2, k)]
    return groups, np.asarray(flat, np.int32)


def _route_kernel(pq_ref, sk_ref, flat_ref, e_ref, g_ref, et_s, gt_s):
    k = PEER_TOPK
    nk = PEER_NKEYS
    tn = ROUTE_TOKENS
    row = lax.broadcasted_iota(I32, (nk, tn), 0)
    groups, _ = _candidate_layout()
    flat = flat_ref[...]

    def half_topk(h, c):
        off = pl.multiple_of((h * 2 + c) * nk, nk)
        s = lax.dot_general(sk_ref[h, c], pq_ref[:, pl.ds(off, nk)], _NT,
                            precision=lax.Precision.HIGHEST, preferred_element_type=F32)
        ts, ti = [], []
        for _ in range(k):
            m = jnp.max(s, axis=0, keepdims=True)
            idx = jnp.min(jnp.where(s == m, row, nk), axis=0, keepdims=True)
            ts.append(m)
            ti.append(idx)
            s = jnp.where(row == idx, -jnp.inf, s)
        return jnp.concatenate(ts, axis=0), jnp.concatenate(ti, axis=0)

    def head(h, carry):
        s0, i0 = half_topk(h, 0)
        s1, i1 = half_topk(h, 1)
        cs, ce = [], []
        for a0, na, nb in groups:
            if na == 1:
                cs.append(s0[a0:a0 + 1] + s1[:nb])
                ce.append(i0[a0:a0 + 1] * nk + i1[:nb])
            else:
                cs.append(s0[a0:a0 + na] + s1[:1])
                ce.append(i0[a0:a0 + na] * nk + i1[:1])
        cand = jnp.where(flat < _BIG, jnp.concatenate(cs, axis=0), -jnp.inf)
        cexp = jnp.concatenate(ce, axis=0)
        bs, be = [], []
        for _ in range(k):
            m = jnp.max(cand, axis=0, keepdims=True)
            idx = jnp.min(jnp.where(cand == m, flat, _BIG), axis=0, keepdims=True)
            hit = flat == idx
            bs.append(m)
            be.append(jnp.max(jnp.where(hit, cexp, -1), axis=0, keepdims=True))
            cand = jnp.where(hit, -jnp.inf, cand)
        best = jnp.concatenate(bs, axis=0)
        p = jnp.exp(best - best[:1])
        r = pl.multiple_of(h * k, k)
        gt_s[pl.ds(r, k), :] = p / jnp.sum(p, axis=0, keepdims=True)
        et_s[pl.ds(r, k), :] = jnp.concatenate(be, axis=0)
        return carry

    lax.fori_loop(0, PEER_HEADS, head, 0, unroll=8)
    e_ref[...] = et_s[...].T
    g_ref[...] = gt_s[...].T


def _route(pq, sub_keys):
    n, qd = pq.shape
    tn = ROUTE_TOKENS
    _, flat = _candidate_layout()
    flat = jnp.asarray(np.broadcast_to(flat[:, None], (flat.shape[0], tn)))
    row = lambda i: (i, 0)
    return pl.pallas_call(
        _route_kernel,
        out_shape=(jax.ShapeDtypeStruct((n, PEER_SLOTS), I32), jax.ShapeDtypeStruct((n, PEER_SLOTS), F32)),
        grid=(n // tn,),
        in_specs=[pl.BlockSpec((tn, qd), row), _resident(sub_keys.shape), _resident(flat.shape)],
        out_specs=(pl.BlockSpec((tn, PEER_SLOTS), row), pl.BlockSpec((tn, PEER_SLOTS), row)),
        scratch_shapes=[pltpu.VMEM((PEER_SLOTS, tn), I32), pltpu.VMEM((PEER_SLOTS, tn), F32)],
        compiler_params=_cparams(1), name="peer_route")(pq, sub_keys, flat)


EXPERT_CHUNK = 1024
_NKEYS_BITS = PEER_NKEYS.bit_length() - 1
assert 1 << _NKEYS_BITS == PEER_NKEYS


def _split_expert(e):
    return lax.shift_right_logical(e, _NKEYS_BITS), e & (PEER_NKEYS - 1)


def _host_call(kernel_fn, name, grid, in_specs, out_specs, out_shape, scratch_shapes, operands, gate_args,
               drain_steps=0):
    n_chunks = grid[1] - drain_steps
    stream = None
    if gate_args is not None:
        stream = _GateStream(*gate_args, n_steps=grid[0] * n_chunks,
                             flat_step=lambda i, c: i * n_chunks + jnp.minimum(c, n_chunks - 1))
        in_specs = in_specs + stream.in_specs()
        out_specs = out_specs + stream.out_specs()
        out_shape = out_shape + stream.out_shape
        operands = [stream.prefetch] + operands + stream.operands
    grid_spec = pltpu.PrefetchScalarGridSpec(
        num_scalar_prefetch=0 if stream is None else 1, grid=grid, in_specs=in_specs, out_specs=out_specs,
        scratch_shapes=scratch_shapes)
    return pl.pallas_call(functools.partial(kernel_fn, stream), out_shape=out_shape, grid_spec=grid_spec,
                          compiler_params=_cparams(2, HOST_VMEM_LIMIT_BYTES), name=name)(*operands)


def _split_host_refs(stream, refs, n_in, n_out):
    if stream is None:
        return refs[:n_in], refs[n_in:n_in + n_out], refs[n_in + n_out:], None
    refs = refs[1:]
    n_gate_in = 1 + stream.pages_per_step
    gate_in = refs[n_in:n_in + n_gate_in]
    outs = refs[n_in + n_gate_in:]
    gate = (gate_in[0], gate_in[1:], outs[n_out], outs[n_out + 1])
    return refs[:n_in], outs[:n_out], outs[n_out + 2:], gate


def _peer_up_kernel(stream, *refs):
    (xn_ref, e_ref, u_ref), (hs_ref,), (hd_s,), gate = _split_host_refs(stream, refs, 3, 1)
    c = pl.program_id(1)
    groups = EXPERT_CHUNK // PEER_NKEYS
    if gate is not None:
        gate_s = stream.start_step((pl.program_id(0), c), gate[2])

    @pl.when(c == 0)
    def _():
        hs_ref[...] = jnp.zeros(hs_ref.shape, F32)
        hd_s[1] = jnp.zeros(hd_s.shape[1:], F32)

    def step(cur):
        if gate is not None:
            _gate_step(gate_s, stream.n_blk, *gate)
        xn = xn_ref[...]
        for j in range(EXPERT_CHUNK // MXU_WIDTH):
            hd_s[cur, :, j * MXU_WIDTH:(j + 1) * MXU_WIDTH] = lax.dot_general(
                xn, u_ref[j * MXU_WIDTH:(j + 1) * MXU_WIDTH, :].astype(BF16), _NT, preferred_element_type=F32)
        e = e_ref[...]
        i1, i2 = _split_expert(e)
        acc = hs_ref[...]
        for g in range(groups):
            got = jnp.take_along_axis(hd_s[1 - cur, :, g * PEER_NKEYS:(g + 1) * PEER_NKEYS], i2, axis=1)
            acc = jnp.where(i1 == (c - 1) * groups + g, got, acc)
        hs_ref[...] = acc

    for parity in range(2):
        pl.when(lax.rem(c, 2) == parity)(functools.partial(step, parity))


def _peer_up(xn_bf16, e, u_tab, gate_args=None):
    n, d = xn_bf16.shape
    n_exp = u_tab.shape[0]
    tn = 1024 if n % 1024 == 0 else (256 if n % 256 == 0 else n)
    n_chunks = n_exp // EXPERT_CHUNK
    tok = lambda i, c, *_: (i, 0)
    return _host_call(
        _peer_up_kernel, "peer_up", (n // tn, n_chunks + 1),
        [pl.BlockSpec((tn, d), tok), pl.BlockSpec((tn, PEER_SLOTS), tok),
         pl.BlockSpec((EXPERT_CHUNK, d), lambda i, c, *_: (jnp.minimum(c, n_chunks - 1), 0))],
        [pl.BlockSpec((tn, PEER_SLOTS), tok)], [jax.ShapeDtypeStruct((n, PEER_SLOTS), F32)],
        [pltpu.VMEM((2, tn, EXPERT_CHUNK), F32)], [xn_bf16, e, u_tab], gate_args, drain_steps=1)


TOKEN_BATCH = 2 * SUBLANES
DOWN_EXPERT_CHUNK = 1024


def _peer_down_kernel(tn, stream, *refs):
    (e_ref, gate_ref, hs_ref, v_ref, h_ref, gf_ref), (y_ref,), (act_s, a_s), block_gate = _split_host_refs(
        stream, refs, 6, 1)
    c = pl.program_id(1)
    nk = PEER_NKEYS
    groups = DOWN_EXPERT_CHUNK // nk
    if block_gate is not None:
        gate_s = stream.start_step((pl.program_id(0), c), block_gate[2])

    @pl.when(c == 0)
    def _():
        y_ref[...] = jnp.zeros(y_ref.shape, F32)
        hv = hs_ref[...]
        gelu = 0.5 * hv * (1.0 + lax.erf(hv * math.sqrt(0.5)))
        act_s[...] = gate_ref[...] * gelu
        sub = lax.broadcasted_iota(I32, (nk, PEER_SLOTS), 0)

        def scatter_token(n):
            e = e_ref[pl.ds(n, 1), :]
            a = act_s[pl.ds(n, 1), :]
            i1, i2 = _split_expert(e)
            pt = jnp.where(sub == i1, a, 0.0).astype(BF16)
            qt = jnp.where(sub == i2, 1.0, 0.0).astype(BF16)
            return lax.dot_general(pt, qt, _NT, preferred_element_type=F32)

        def batch(t, carry):
            n0 = pl.multiple_of(t * TOKEN_BATCH, TOKEN_BATCH)
            halves = []
            for s0 in range(0, TOKEN_BATCH, SUBLANES):
                per_token = jnp.stack([scatter_token(n0 + s0 + k) for k in range(SUBLANES)], axis=0)
                halves.append(jnp.swapaxes(per_token, 0, 1))
            a_s[:, pl.ds(n0, TOKEN_BATCH), :] = jnp.concatenate(halves, axis=1).astype(BF16)
            return carry

        lax.fori_loop(0, tn // TOKEN_BATCH, batch, 0)

    if block_gate is not None:
        _gate_step(gate_s, stream.n_blk, *block_gate)

    lhs = jnp.concatenate([a_s[c * groups + g] for g in range(groups)], axis=1)
    y_ref[...] += jnp.dot(lhs, v_ref[...], preferred_element_type=F32)

    @pl.when(c == pl.num_programs(1) - 1)
    def _():
        y_ref[...] = _rmsnorm(h_ref[...] + y_ref[...], gf_ref[...])


def _peer_down(e, gate, hs, v_bf16, h, g_final, gate_args=None):
    n, d = h.shape
    n_exp = v_bf16.shape[0]
    tn = 512 if n % 512 == 0 else n
    tok = lambda i, c, *_: (i, 0)
    return _host_call(
        functools.partial(_peer_down_kernel, tn), "peer_down", (n // tn, n_exp // DOWN_EXPERT_CHUNK),
        [pl.BlockSpec((tn, PEER_SLOTS), tok), pl.BlockSpec((tn, PEER_SLOTS), tok),
         pl.BlockSpec((tn, PEER_SLOTS), tok),
         pl.BlockSpec((DOWN_EXPERT_CHUNK, d), lambda i, c, *_: (c, 0)),
         pl.BlockSpec((tn, d), tok, pipeline_mode=pl.Buffered(1)),
         pl.BlockSpec((1, d), lambda i, c, *_: (0, 0), pipeline_mode=pl.Buffered(1))],
        [pl.BlockSpec((tn, d), tok)], [jax.ShapeDtypeStruct((n, d), F32)],
        [pltpu.VMEM((tn, PEER_SLOTS), F32), pltpu.VMEM((PEER_NKEYS, tn, PEER_NKEYS), BF16)],
        [e, gate, hs, v_bf16, h, g_final.reshape(1, d)], gate_args)


def _peer_and_final_norm(h, xn_bf16, pq, sub_keys, u_tab, v_bf16, g_final, gate_args=None):
    n = h.shape[0]
    n_pad = -(-n // ROUTE_TOKENS) * ROUTE_TOKENS
    pq_pad = pq if n_pad == n else jnp.pad(pq, ((0, n_pad - n), (0, 0)))
    e, gate = _route(pq_pad, sub_keys)
    e, gate = e[:n], gate[:n]
    if gate_args is None:
        (hs,) = _peer_up(xn_bf16, e, u_tab)
        (y,) = _peer_down(e, gate, hs, v_bf16, h, g_final)
        return y, None
    q_sample, cache_k, page_table = gate_args
    half = q_sample.shape[0] // 2
    hs, _, top_a = _peer_up(xn_bf16, e, u_tab, (q_sample[:half], cache_k, page_table[:half]))
    y, _, top_b = _peer_down(e, gate, hs, v_bf16, h, g_final, (q_sample[half:], cache_k, page_table[half:]))
    return y, _GateStream.top_blocks([top_a, top_b])


def kernel(x_prompt, x_sample, cache_k, cache_v, state_conv, page_table, rel_bias_table, norm_mix_g, w_in,
           w_dw, b_dw, conv_ln_g, conv_ln_b, w_out, norm_ffn_g, peer_wq, peer_sub_keys, peer_u, peer_v,
           final_norm_g):
    batch, seq, d = x_prompt.shape
    bs, s_new, _ = x_sample.shape
    depth = w_in.shape[0]
    assert s_new == 1 and depth == 1 and seq % MOBA_BLOCK == 0 and seq // MOBA_BLOCK >= MOBA_TOPK
    hist = CONV_WIDTH - 1
    blk = MOBA_BLOCK

    dist = np.arange(blk)[None, :] - np.arange(blk)[:, None]
    bkt_prompt = jnp.concatenate([_rel_bucket(dist), _rel_bucket(dist + blk)], axis=0)
    bias_prompt = _bias_lookup(rel_bias_table, bkt_prompt).reshape(N_HEADS, 2, blk, blk)
    bkt_last = jnp.broadcast_to(_rel_bucket(blk - np.arange(blk))[None, :], (SUBLANES, blk))
    bias_last = _bias_lookup(rel_bias_table, bkt_last)[:, 0, :]

    hp = x_prompt.reshape(batch * seq, d)
    hs_ = x_sample.reshape(bs, d)
    l = 0
    w_in_b = w_in[l].astype(BF16)
    w_out_b = w_out[l].astype(BF16)
    wq_b = peer_wq[l].astype(BF16)
    u_b = peer_u[l]
    v_b = peer_v[l].astype(BF16)

    qp, kp, vp, up = _proj(hp, norm_mix_g[l], w_in_b)
    qs, ks, vs, us = _proj(hs_, norm_mix_g[l], w_in_b)
    q3 = qs.reshape(bs, N_HEADS, HEAD_DIM)

    attn_p = _prompt_attention(qp, kp, vp, bias_prompt, rel_bias_table, batch, seq)
    conv_p = _conv_prompt(up, w_dw[l], b_dw[l], conv_ln_g[l], conv_ln_b[l], batch, seq)
    h_p, xn_p, pq_p = _mix_out(attn_p, conv_p, hp, w_out_b, norm_ffn_g[l], wq_b)
    y_p, top = _peer_and_final_norm(h_p, xn_p, pq_p, peer_sub_keys[l], u_b, v_b, final_norm_g,
                                    (q3, cache_k[l], page_table))

    attn_s = _sample_attention(q3, ks.reshape(q3.shape), vs.reshape(q3.shape), cache_k[l], cache_v[l],
                               page_table, top, rel_bias_table, bias_last)
    conv_s, state_new = _conv_step(state_conv[l], us, w_dw[l], b_dw[l], conv_ln_g[l], conv_ln_b[l])
    h_s, xn_s, pq_s = _mix_out(attn_s, conv_s, hs_, w_out_b, norm_ffn_g[l], wq_b)
    y_s, _ = _peer_and_final_norm(h_s, xn_s, pq_s, peer_sub_keys[l], u_b, v_b, final_norm_g)

    kv_p = (1, batch, seq, N_HEADS, HEAD_DIM)
    kv_s = (1, bs, 1, N_HEADS, HEAD_DIM)
    conv_prompt_new = up.reshape(batch, seq, -1)[:, seq - hist:, :][None]
    return (y_p.reshape(batch, seq, d), y_s.reshape(bs, 1, d),
            kp.reshape(kv_p), vp.reshape(kv_p), conv_prompt_new,
            ks.reshape(kv_s), vs.reshape(kv_s), state_new[None])
```

```python
import functools
import math

import numpy as np
import jax
import jax.numpy as jnp
from jax import lax
from jax.experimental import pallas as pl
from jax.experimental.pallas import tpu as pltpu

F32, BF16, I32 = jnp.float32, jnp.bfloat16, jnp.int32

N_HEADS = 8
HEAD_DIM = 128
ATTN_WIDTH = N_HEADS * HEAD_DIM
MOBA_BLOCK = 256
MOBA_TOPK = 3
PAGE_SIZE = 128
REL_BUCKETS = 32
REL_MAX_DIST = 128
CONV_WIDTH = 31
PEER_HEADS = 8
PEER_NKEYS = 128
PEER_TOPK = 16
PEER_SLOTS = PEER_HEADS * PEER_TOPK
EPS = 1e-6
NEG = -1e30
SCALE = HEAD_DIM ** -0.5

LANES = 128
SUBLANES = 8
MXU_WIDTH = 256
VMEM_LIMIT_BYTES = 56 * 1024 * 1024
HOST_VMEM_LIMIT_BYTES = 60 * 1024 * 1024

_NT = (((1,), (1,)), ((), ()))


def _cparams(n_axes, vmem_limit_bytes=VMEM_LIMIT_BYTES):
    return pltpu.CompilerParams(dimension_semantics=("arbitrary",) * n_axes,
                                vmem_limit_bytes=vmem_limit_bytes)


def _resident(shape):
    nd = len(shape)
    return pl.BlockSpec(shape, lambda *_: (0,) * nd, pipeline_mode=pl.Buffered(1))


def _rmsnorm(x, g):
    return x * lax.rsqrt(jnp.mean(x * x, axis=-1, keepdims=True) + EPS) * g


def _proj_kernel(x_ref, g_ref, w_ref, q_ref, k_ref, v_ref, u_ref):
    xn = _rmsnorm(x_ref[...], g_ref[...]).astype(BF16)
    c = ATTN_WIDTH

    def mm(j):
        return jnp.dot(xn, w_ref[:, j * c:(j + 1) * c], preferred_element_type=F32)

    q_ref[...] = mm(0)
    k_ref[...] = mm(1)
    v_ref[...] = mm(2)
    u_ref[...] = mm(3) * jax.nn.sigmoid(mm(4))


def _proj(x, g, w_bf16):
    m, d = x.shape
    tm = min(m, 256)
    row = lambda i: (i, 0)
    out = jax.ShapeDtypeStruct((m, ATTN_WIDTH), F32)
    return pl.pallas_call(
        _proj_kernel, out_shape=(out,) * 4, grid=(m // tm,),
        in_specs=[pl.BlockSpec((tm, d), row), _resident((1, d)), _resident(w_bf16.shape)],
        out_specs=(pl.BlockSpec((tm, ATTN_WIDTH), row),) * 4,
        compiler_params=_cparams(1), name="proj")(x, g.reshape(1, d), w_bf16)


MASKED_BUCKET = REL_BUCKETS


def _rel_bucket(dist):
    dist = jnp.asarray(dist, I32)
    n = jnp.maximum(dist, 0)
    max_exact = REL_BUCKETS // 2
    nf = jnp.maximum(n, 1).astype(F32)
    large = max_exact + (jnp.log(nf / max_exact) / math.log(REL_MAX_DIST / max_exact)
                         * (REL_BUCKETS - max_exact)).astype(I32)
    bucket = jnp.where(n < max_exact, n, jnp.minimum(large, REL_BUCKETS - 1))
    return jnp.where(dist < 0, MASKED_BUCKET, bucket)


def _bias_kernel(tab_ref, bkt_ref, o_ref):
    h = pl.program_id(0)
    bkt = bkt_ref[...]
    acc = jnp.full(bkt.shape, NEG, F32)
    for b in range(REL_BUCKETS):
        acc = jnp.where(bkt == b, tab_ref[b, h], acc)
    o_ref[...] = acc


def _bias_lookup(rel_table, bkt):
    r, c = bkt.shape
    return pl.pallas_call(
        _bias_kernel, out_shape=jax.ShapeDtypeStruct((N_HEADS, r, c), F32), grid=(N_HEADS,),
        in_specs=[pl.BlockSpec(memory_space=pltpu.SMEM), _resident((r, c))],
        out_specs=pl.BlockSpec((None, r, c), lambda h: (h, 0, 0)),
        compiler_params=_cparams(1), name="rel_bias")(rel_table, bkt)


ATTN_WIDE_GROUP = 4
PAGES_PER_BLOCK = MOBA_BLOCK // PAGE_SIZE


class _GateStream:
    def __init__(self, q_sample, cache_k, page_table, n_steps, flat_step):
        bs, n_pages = page_table.shape
        assert (bs * n_pages) % n_steps == 0
        self.pages_per_step = bs * n_pages // n_steps
        assert n_pages % self.pages_per_step == 0 and self.pages_per_step % PAGES_PER_BLOCK == 0
        self.steps_per_seq = n_pages // self.pages_per_step
        self.n_blk = n_pages // PAGES_PER_BLOCK
        assert self.n_blk >= MOBA_TOPK
        self.flat_step = flat_step
        self.operands = [q_sample] + [cache_k] * self.pages_per_step
        self.prefetch = page_table
        self.out_shape = [jax.ShapeDtypeStruct((bs, N_HEADS, self.n_blk), F32),
                          jax.ShapeDtypeStruct((bs, N_HEADS, LANES), I32)]

    def _seq(self, idx):
        return self.flat_step(*idx) // self.steps_per_seq

    def in_specs(self):
        def page_spec(r):
            def index(*a):
                f = self.flat_step(*a[:-1])
                return (a[-1][f // self.steps_per_seq, (f % self.steps_per_seq) * self.pages_per_step + r],
                        0, 0, 0)
            return pl.BlockSpec((None, PAGE_SIZE, N_HEADS, HEAD_DIM), index)

        return ([pl.BlockSpec((None, N_HEADS, HEAD_DIM), lambda *a: (self._seq(a[:-1]), 0, 0))]
                + [page_spec(r) for r in range(self.pages_per_step)])

    def out_specs(self):
        per_seq = lambda *a: (self._seq(a[:-1]), 0, 0)
        return [pl.BlockSpec((None, N_HEADS, self.n_blk), per_seq),
                pl.BlockSpec((None, N_HEADS, LANES), per_seq)]

    def start_step(self, grid_idx, g_ref):
        s = lax.rem(self.flat_step(*grid_idx), self.steps_per_seq)

        @pl.when(s == 0)
        def _():
            g_ref[...] = jnp.zeros(g_ref.shape, F32)

        return s

    @staticmethod
    def top_blocks(tops):
        top = jnp.concatenate(tops, axis=0)
        return top[:, :, :MOBA_TOPK].reshape(top.shape[0], N_HEADS * MOBA_TOPK)


def _gate_step(s, n_blk, q_ref, pages, g_ref, top_ref):
    blocks_per_step = len(pages) // PAGES_PER_BLOCK
    q = q_ref[...]
    lane = lax.broadcasted_iota(I32, (N_HEADS, n_blk), 1)
    acc = g_ref[...]
    for p in range(blocks_per_step):
        bsum = functools.reduce(jnp.add, [jnp.sum(pages[PAGES_PER_BLOCK * p + r][...], axis=0)
                                          for r in range(PAGES_PER_BLOCK)])
        gn = jnp.sum(q * (bsum * (1.0 / MOBA_BLOCK)), axis=1, keepdims=True)
        acc = jnp.where(lane == s * blocks_per_step + p, gn, acc)
    g_ref[...] = acc

    g = acc
    out_lane = lax.broadcasted_iota(I32, (N_HEADS, LANES), 1)
    top = jnp.zeros((N_HEADS, LANES), I32)
    for t in range(MOBA_TOPK):
        m = jnp.max(g, axis=1, keepdims=True)
        idx = jnp.min(jnp.where(g == m, lane, n_blk), axis=1, keepdims=True)
        top = jnp.where(out_lane == t, idx, top)
        g = jnp.where(lane == idx, -jnp.inf, g)
    top_ref[...] = top


def _attn_kernel(nb, tab_ref, q_ref, k_ref, v_ref, bias_ref, o_ref,
                 kb_s, vt_s, km_s, far_s, m_s, l_s, acc_s):
    h = pl.program_id(1)
    i = pl.program_id(2)
    blk = MOBA_BLOCK

    @pl.when(i == 0)
    def _():
        kb_s[...] = k_ref[...].astype(BF16)
        for j in range(nb):
            rows = slice(j * blk, (j + 1) * blk)
            km_s[j:j + 1, :] = jnp.mean(k_ref[rows, :], axis=0, keepdims=True)
            vt_s[:, rows] = v_ref[rows, :].T.astype(BF16)

    q = q_ref[...]
    qt = q.T.astype(BF16)

    gate = lax.dot_general(km_s[...], q, _NT, precision=lax.Precision.HIGHEST,
                           preferred_element_type=F32)
    sub = lax.broadcasted_iota(I32, (nb, blk), 0)
    gate = jnp.where(sub < i, gate, NEG)
    rank = jnp.zeros((nb, blk), I32)
    for jp in range(nb):
        gj = gate[jp:jp + 1, :]
        rank = rank + jnp.where(gj == gate, jnp.where(jp < sub, 1, 0),
                                jnp.where(gj > gate, 1, 0))
    selneg = jnp.where((rank < MOBA_TOPK) & (sub < i), 0.0, NEG)
    far = selneg + tab_ref[REL_BUCKETS - 1, h]

    def row_of(x, j):
        return jnp.where(j >= 0, jnp.sum(jnp.where(sub == j, x, 0.0), axis=0, keepdims=True), NEG)

    n_first = ATTN_WIDE_GROUP
    n_rest = jnp.maximum(i + 1 - n_first, 0)
    far_s[...] = jnp.where(sub < n_rest, far, NEG)

    def partial_softmax(blocks):
        rows = [pl.multiple_of(j * blk, blk) for j, _ in blocks]
        sts = [jnp.dot(kb_s[pl.ds(r, blk), :], qt, preferred_element_type=F32) * SCALE + extra
               for r, (_, extra) in zip(rows, blocks)]
        ms = [jnp.max(st, axis=0, keepdims=True) for st in sts]
        ps = [jnp.exp(st - m) for st, m in zip(sts, ms)]
        ls = [jnp.sum(p, axis=0, keepdims=True) for p in ps]
        os_ = [jnp.dot(vt_s[:, pl.ds(r, blk)], p.astype(BF16), preferred_element_type=F32)
               for r, p in zip(rows, ps)]
        return list(zip(ms, ls, os_))

    def merge(parts):
        m = functools.reduce(jnp.maximum, [pm for pm, _, _ in parts])
        w = [jnp.exp(pm - m) for pm, _, _ in parts]
        l = functools.reduce(jnp.add, [wi * pl_ for wi, (_, pl_, _) in zip(w, parts)])
        o = functools.reduce(jnp.add, [wi * po for wi, (_, _, po) in zip(w, parts)])
        return m, l, o

    first = [(i, bias_ref[0]), (jnp.maximum(i - 1, 0), bias_ref[1] + row_of(selneg, i - 1))]
    first += [(jnp.maximum(i - back, 0), row_of(far, i - back)) for back in range(2, n_first)]
    m0, l0, o0 = merge(partial_softmax(first))
    m_s[...] = m0
    l_s[...] = l0
    acc_s[...] = o0

    def group(g, carry):
        j0 = g * ATTN_WIDE_GROUP
        parts = partial_softmax([(j0 + u, far_s[pl.ds(j0 + u, 1), :]) for u in range(ATTN_WIDE_GROUP)])
        m, l, o = merge([(m_s[...], l_s[...], acc_s[...])] + parts)
        m_s[...] = m
        l_s[...] = l
        acc_s[...] = o
        return carry

    lax.fori_loop(0, (n_rest + ATTN_WIDE_GROUP - 1) // ATTN_WIDE_GROUP, group, 0)

    o_ref[...] = (acc_s[...] / l_s[...]).T


def _prompt_attention(q, k, v, bias, rel_table, batch, seq):
    nb = seq // MOBA_BLOCK
    blk = MOBA_BLOCK
    qo = lambda b, h, i: (b * nb + i, h)
    kv = lambda b, h, i: (b, h)
    return pl.pallas_call(
        functools.partial(_attn_kernel, nb),
        out_shape=jax.ShapeDtypeStruct((batch * seq, ATTN_WIDTH), F32),
        grid=(batch, N_HEADS, nb),
        in_specs=[pl.BlockSpec(memory_space=pltpu.SMEM),
                  pl.BlockSpec((blk, HEAD_DIM), qo),
                  pl.BlockSpec((seq, HEAD_DIM), kv),
                  pl.BlockSpec((seq, HEAD_DIM), kv),
                  pl.BlockSpec((None, 2, blk, blk), lambda b, h, i: (h, 0, 0, 0))],
        out_specs=pl.BlockSpec((blk, HEAD_DIM), qo),
        scratch_shapes=[pltpu.VMEM((seq, HEAD_DIM), BF16), pltpu.VMEM((HEAD_DIM, seq), BF16),
                        pltpu.VMEM((nb, HEAD_DIM), F32), pltpu.VMEM((nb, blk), F32),
                        pltpu.VMEM((1, blk), F32), pltpu.VMEM((1, blk), F32),
                        pltpu.VMEM((HEAD_DIM, blk), F32)],
        compiler_params=_cparams(3), name="moba_prompt")(rel_table, q, k, v, bias)


PAGES_PER_HEAD = MOBA_TOPK * PAGES_PER_BLOCK
PAGES_PER_SEQ = N_HEADS * PAGES_PER_HEAD


def _sattn_kernel(n_blk, pt_ref, top_ref, tab_ref, q_ref, kn_ref, vn_ref, b_last_ref, ck_hbm, cv_hbm,
                  o_ref, kbuf, vbuf, ksem, vsem):
    b = pl.program_id(0)
    slot = lax.rem(b, 2)

    def page_copies(seq, sl):
        out = []
        for h in range(N_HEADS):
            for t in range(MOBA_TOPK):
                blk = top_ref[seq, h * MOBA_TOPK + t]
                for r in range(PAGES_PER_BLOCK):
                    page = pt_ref[seq, PAGES_PER_BLOCK * blk + r]
                    idx = h * PAGES_PER_HEAD + t * PAGES_PER_BLOCK + r
                    out.append(pltpu.make_async_copy(ck_hbm.at[page, :, h, :], kbuf.at[sl, idx], ksem.at[sl]))
                    out.append(pltpu.make_async_copy(cv_hbm.at[page, :, h, :], vbuf.at[sl, idx], vsem.at[sl]))
        return out

    def start_all(copies):
        for n, cp in enumerate(copies):
            cp.start(priority=n % 2)

    @pl.when(b == 0)
    def _():
        start_all(page_copies(0, 0))

    @pl.when(b + 1 < pl.num_programs(0))
    def _():
        start_all(page_copies(b + 1, 1 - slot))

    for cp in page_copies(b, slot):
        cp.wait()

    q = q_ref[...]
    vn = vn_ref[...]
    s_self = jnp.sum(q * kn_ref[...], axis=1, keepdims=True) * SCALE
    sub = lax.broadcasted_iota(I32, (N_HEADS, HEAD_DIM), 0)
    out = jnp.zeros((N_HEADS, HEAD_DIM), F32)
    n_keys = PAGES_PER_HEAD * PAGE_SIZE
    for h in range(N_HEADS):
        pages = slice(h * PAGES_PER_HEAD, (h + 1) * PAGES_PER_HEAD)
        kh = kbuf[slot, pages].reshape(n_keys, HEAD_DIM).astype(BF16)
        vh = vbuf[slot, pages].reshape(n_keys, HEAD_DIM).astype(BF16)
        q8 = jnp.broadcast_to(q[h:h + 1, :], (SUBLANES, HEAD_DIM)).astype(BF16)
        s = lax.dot_general(q8, kh, _NT, preferred_element_type=F32)[:1, :] * SCALE
        far_bias = tab_ref[REL_BUCKETS - 1, h]
        bias = []
        for t in range(MOBA_TOPK):
            is_last = top_ref[b, h * MOBA_TOPK + t] == n_blk - 1
            bias.append(jnp.where(is_last, b_last_ref[h:h + 1, :], far_bias))
        s = s + jnp.concatenate(bias, axis=1)
        sh = s_self[h:h + 1, :] + tab_ref[0, h]
        m = jnp.maximum(jnp.max(s, axis=1, keepdims=True), sh)
        p = jnp.exp(s - m)
        p_self = jnp.exp(sh - m)
        l = jnp.sum(p, axis=1, keepdims=True) + p_self
        p8 = jnp.broadcast_to(p, (SUBLANES, n_keys)).astype(BF16)
        o = jnp.dot(p8, vh, preferred_element_type=F32)[:1, :] + p_self * vn[h:h + 1, :]
        out = jnp.where(sub == h, o / l, out)
    o_ref[...] = out


def _sample_attention(q3, kn3, vn3, cache_k, cache_v, page_table, top, rel_table, bias_last):
    bs = q3.shape[0]
    n_blk = page_table.shape[1] * PAGE_SIZE // MOBA_BLOCK
    per_b = pl.BlockSpec((None, N_HEADS, HEAD_DIM), lambda b, pt, tp: (b, 0, 0))
    buf = pltpu.VMEM((2, PAGES_PER_SEQ, PAGE_SIZE, HEAD_DIM), F32)
    grid_spec = pltpu.PrefetchScalarGridSpec(
        num_scalar_prefetch=2, grid=(bs,),
        in_specs=[pl.BlockSpec(memory_space=pltpu.SMEM), per_b, per_b, per_b,
                  pl.BlockSpec((N_HEADS, MOBA_BLOCK), lambda b, pt, tp: (0, 0)),
                  pl.BlockSpec(memory_space=pl.ANY), pl.BlockSpec(memory_space=pl.ANY)],
        out_specs=per_b,
        scratch_shapes=[buf, buf, pltpu.SemaphoreType.DMA((2,)), pltpu.SemaphoreType.DMA((2,))])
    out = pl.pallas_call(
        functools.partial(_sattn_kernel, n_blk),
        out_shape=jax.ShapeDtypeStruct((bs, N_HEADS, HEAD_DIM), F32),
        grid_spec=grid_spec, compiler_params=_cparams(1), name="moba_sample")(
            page_table, top, rel_table, q3, kn3, vn3, bias_last, cache_k, cache_v)
    return out.reshape(bs, ATTN_WIDTH)


CONV_HALO = 32
CONV_ROWS = 64


def _ln_swish(y, g, b):
    mu = jnp.mean(y, axis=-1, keepdims=True)
    var = jnp.mean(jnp.square(y - mu), axis=-1, keepdims=True)
    yn = (y - mu) * lax.rsqrt(var + EPS) * g + b
    return yn * jax.nn.sigmoid(yn)


def _conv_kernel(tt, cur_ref, halo_ref, w_ref, b_ref, lg_ref, lb_ref, y_ref, ext_s, yc_s):
    t = pl.program_id(1)
    ch = cur_ref.shape[1]
    ext_s[CONV_HALO:CONV_HALO + tt, :] = cur_ref[...]
    ext_s[CONV_HALO + tt:, :] = jnp.zeros((SUBLANES, ch), F32)

    @pl.when(t == 0)
    def _():
        ext_s[:CONV_HALO, :] = jnp.zeros((CONV_HALO, ch), F32)

    @pl.when(t > 0)
    def _():
        ext_s[:CONV_HALO, :] = halo_ref[...]

    lead = CONV_HALO - (CONV_WIDTH - 1)
    rc = CONV_ROWS

    def chunk(r, carry):
        r0 = pl.multiple_of(r * rc, rc)
        for lt in range(ch // LANES):
            ls = slice(lt * LANES, (lt + 1) * LANES)
            win = ext_s[pl.ds(r0, rc + CONV_HALO + SUBLANES), ls]
            acc = None
            for sh in range(SUBLANES):
                z = None
                for a in range(CONV_HALO // SUBLANES + 1):
                    k = SUBLANES * a + sh - lead
                    if 0 <= k < CONV_WIDTH:
                        term = w_ref[k:k + 1, ls] * win[SUBLANES * a:SUBLANES * a + rc + SUBLANES]
                        z = term if z is None else z + term
                z = z[sh:sh + rc]
                acc = z if acc is None else acc + z
            yc_s[pl.ds(r0, rc), ls] = acc + b_ref[:, ls]
        return carry

    lax.fori_loop(0, tt // rc, chunk, 0)
    y_ref[...] = _ln_swish(yc_s[...], lg_ref[...], lb_ref[...])


def _conv_prompt(u, w_dw, b_dw, ln_g, ln_b, batch, seq):
    ch = u.shape[1]
    tt = min(seq, 512)
    nt = seq // tt
    hpt = tt // CONV_HALO
    vec = lambda x: x.reshape(1, ch)
    return pl.pallas_call(
        functools.partial(_conv_kernel, tt),
        out_shape=jax.ShapeDtypeStruct((batch * seq, ch), F32), grid=(batch, nt),
        in_specs=[pl.BlockSpec((tt, ch), lambda b, t: (b * nt + t, 0)),
                  pl.BlockSpec((CONV_HALO, ch), lambda b, t: (jnp.maximum((b * nt + t) * hpt - 1, 0), 0)),
                  _resident(w_dw.shape), _resident((1, ch)), _resident((1, ch)), _resident((1, ch))],
        out_specs=pl.BlockSpec((tt, ch), lambda b, t: (b * nt + t, 0)),
        scratch_shapes=[pltpu.VMEM((tt + CONV_HALO + SUBLANES, ch), F32), pltpu.VMEM((tt, ch), F32)],
        compiler_params=_cparams(2), name="conv_prompt")(u, u, w_dw, vec(b_dw), vec(ln_g), vec(ln_b))


def _conv_step_kernel(st_ref, u_ref, w_ref, b_ref, lg_ref, lb_ref, y_ref, ns_ref):
    hist = CONV_WIDTH - 1
    u = u_ref[...]
    acc = w_ref[hist:hist + 1, :] * u
    for k in range(hist):
        acc = acc + w_ref[k:k + 1, :] * st_ref[:, k, :]
    y_ref[...] = _ln_swish(acc + b_ref[...], lg_ref[...], lb_ref[...])
    for k in range(hist - 1):
        ns_ref[:, k, :] = st_ref[:, k + 1, :]
    ns_ref[:, hist - 1, :] = u


def _conv_step(state, u, w_dw, b_dw, ln_g, ln_b):
    bs, hist, ch = state.shape
    vec = lambda x: x.reshape(1, ch)
    return pl.pallas_call(
        _conv_step_kernel,
        out_shape=(jax.ShapeDtypeStruct((bs, ch), F32), jax.ShapeDtypeStruct((bs, hist, ch), F32)),
        compiler_params=pltpu.CompilerParams(vmem_limit_bytes=VMEM_LIMIT_BYTES),
        name="conv_step")(state, u, w_dw, vec(b_dw), vec(ln_g), vec(ln_b))


def _mix_kernel(attn_ref, conv_ref, x_ref, wo_ref, g_ref, wq_ref, h_ref, xn_ref, pq_ref):
    mix = jnp.concatenate([attn_ref[...], conv_ref[...]], axis=1).astype(BF16)
    h = x_ref[...] + jnp.dot(mix, wo_ref[...], preferred_element_type=F32)
    h_ref[...] = h
    xn = _rmsnorm(h, g_ref[...]).astype(BF16)
    xn_ref[...] = xn
    pq_ref[...] = jnp.dot(xn, wq_ref[...], preferred_element_type=F32)


def _mix_out(attn, conv, x, wo_bf16, g, wq_bf16):
    m, d = x.shape
    tm = min(m, 256)
    row = lambda i: (i, 0)
    half = attn.shape[1]
    return pl.pallas_call(
        _mix_kernel,
        out_shape=(jax.ShapeDtypeStruct((m, d), F32), jax.ShapeDtypeStruct((m, d), BF16),
                   jax.ShapeDtypeStruct((m, wq_bf16.shape[1]), F32)),
        grid=(m // tm,),
        in_specs=[pl.BlockSpec((tm, half), row), pl.BlockSpec((tm, half), row), pl.BlockSpec((tm, d), row),
                  _resident(wo_bf16.shape), _resident((1, d)), _resident(wq_bf16.shape)],
        out_specs=(pl.BlockSpec((tm, d), row), pl.BlockSpec((tm, d), row),
                   pl.BlockSpec((tm, wq_bf16.shape[1]), row)),
        compiler_params=_cparams(1), name="mix_out")(attn, conv, x, wo_bf16, g.reshape(1, d), wq_bf16)


ROUTE_TOKENS = LANES
_BIG = 1 << 20


def _candidate_layout():
    k = PEER_TOPK
    groups, flat = [], []
    for a in range(k // 2):
        nb = k // (a + 1)
        rows = -(-nb // SUBLANES) * SUBLANES
        groups.append((a, 1, rows))
        flat += [a * k + b if b < nb else _BIG for b in range(rows)]
    groups.append((k // 2, k // 2, 1))
    flat += [a * k for a in range(k // 2, k)]
    return groups, np.asarray(flat, np.int32)


def _route_kernel(pq_ref, sk_ref, flat_ref, e_ref, g_ref, et_s, gt_s):
    k = PEER_TOPK
    nk = PEER_NKEYS
    tn = ROUTE_TOKENS
    row = lax.broadcasted_iota(I32, (nk, tn), 0)
    groups, _ = _candidate_layout()
    flat = flat_ref[...]

    def half_topk(h, c):
        off = pl.multiple_of((h * 2 + c) * nk, nk)
        s = lax.dot_general(sk_ref[h, c], pq_ref[:, pl.ds(off, nk)], _NT,
                            precision=lax.Precision.HIGHEST, preferred_element_type=F32)
        ts, ti = [], []
        for _ in range(k):
            m = jnp.max(s, axis=0, keepdims=True)
            idx = jnp.min(jnp.where(s == m, row, nk), axis=0, keepdims=True)
            ts.append(m)
            ti.append(idx)
            s = jnp.where(row == idx, -jnp.inf, s)
        return jnp.concatenate(ts, axis=0), jnp.concatenate(ti, axis=0)

    def head(h, carry):
        s0, i0 = half_topk(h, 0)
        s1, i1 = half_topk(h, 1)
        cs, ce = [], []
        for a0, na, nb in groups:
            if na == 1:
                cs.append(s0[a0:a0 + 1] + s1[:nb])
                ce.append(i0[a0:a0 + 1] * nk + i1[:nb])
            else:
                cs.append(s0[a0:a0 + na] + s1[:1])
                ce.append(i0[a0:a0 + na] * nk + i1[:1])
        cand = jnp.where(flat < _BIG, jnp.concatenate(cs, axis=0), -jnp.inf)
        cexp = jnp.concatenate(ce, axis=0)
        bs, be = [], []
        for _ in range(k):
            m = jnp.max(cand, axis=0, keepdims=True)
            idx = jnp.min(jnp.where(cand == m, flat, _BIG), axis=0, keepdims=True)
            hit = flat == idx
            bs.append(m)
            be.append(jnp.max(jnp.where(hit, cexp, -1), axis=0, keepdims=True))
            cand = jnp.where(hit, -jnp.inf, cand)
        best = jnp.concatenate(bs, axis=0)
        p = jnp.exp(best - best[:1])
        r = pl.multiple_of(h * k, k)
        gt_s[pl.ds(r, k), :] = p / jnp.sum(p, axis=0, keepdims=True)
        et_s[pl.ds(r, k), :] = jnp.concatenate(be, axis=0)
        return carry

    lax.fori_loop(0, PEER_HEADS, head, 0, unroll=8)
    e_ref[...] = et_s[...].T
    g_ref[...] = gt_s[...].T


def _route(pq, sub_keys):
    n, qd = pq.shape
    tn = ROUTE_TOKENS
    _, flat = _candidate_layout()
    flat = jnp.asarray(np.broadcast_to(flat[:, None], (flat.shape[0], tn)))
    row = lambda i: (i, 0)
    return pl.pallas_call(
        _route_kernel,
        out_shape=(jax.ShapeDtypeStruct((n, PEER_SLOTS), I32), jax.ShapeDtypeStruct((n, PEER_SLOTS), F32)),
        grid=(n // tn,),
        in_specs=[pl.BlockSpec((tn, qd), row), _resident(sub_keys.shape), _resident(flat.shape)],
        out_specs=(pl.BlockSpec((tn, PEER_SLOTS), row), pl.BlockSpec((tn, PEER_SLOTS), row)),
        scratch_shapes=[pltpu.VMEM((PEER_SLOTS, tn), I32), pltpu.VMEM((PEER_SLOTS, tn), F32)],
        compiler_params=_cparams(1), name="peer_route")(pq, sub_keys, flat)


EXPERT_CHUNK = 1024
_NKEYS_BITS = PEER_NKEYS.bit_length() - 1
assert 1 << _NKEYS_BITS == PEER_NKEYS


def _split_expert(e):
    return lax.shift_right_logical(e, _NKEYS_BITS), e & (PEER_NKEYS - 1)


def _host_call(kernel_fn, name, grid, in_specs, out_specs, out_shape, scratch_shapes, operands, gate_args,
               drain_steps=0):
    n_chunks = grid[1] - drain_steps
    stream = None
    if gate_args is not None:
        stream = _GateStream(*gate_args, n_steps=grid[0] * n_chunks,
                             flat_step=lambda i, c: i * n_chunks + jnp.minimum(c, n_chunks - 1))
        in_specs = in_specs + stream.in_specs()
        out_specs = out_specs + stream.out_specs()
        out_shape = out_shape + stream.out_shape
        operands = [stream.prefetch] + operands + stream.operands
    grid_spec = pltpu.PrefetchScalarGridSpec(
        num_scalar_prefetch=0 if stream is None else 1, grid=grid, in_specs=in_specs, out_specs=out_specs,
        scratch_shapes=scratch_shapes)
    return pl.pallas_call(functools.partial(kernel_fn, stream), out_shape=out_shape, grid_spec=grid_spec,
                          compiler_params=_cparams(2, HOST_VMEM_LIMIT_BYTES), name=name)(*operands)


def _split_host_refs(stream, refs, n_in, n_out):
    if stream is None:
        return refs[:n_in], refs[n_in:n_in + n_out], refs[n_in + n_out:], None
    refs = refs[1:]
    n_gate_in = 1 + stream.pages_per_step
    gate_in = refs[n_in:n_in + n_gate_in]
    outs = refs[n_in + n_gate_in:]
    gate = (gate_in[0], gate_in[1:], outs[n_out], outs[n_out + 1])
    return refs[:n_in], outs[:n_out], outs[n_out + 2:], gate


def _peer_up_kernel(stream, *refs):
    (xn_ref, e_ref, u_ref), (hs_ref,), (hd_s,), gate = _split_host_refs(stream, refs, 3, 1)
    c = pl.program_id(1)
    groups = EXPERT_CHUNK // PEER_NKEYS
    if gate is not None:
        gate_s = stream.start_step((pl.program_id(0), c), gate[2])

    @pl.when(c == 0)
    def _():
        hs_ref[...] = jnp.zeros(hs_ref.shape, F32)
        hd_s[1] = jnp.zeros(hd_s.shape[1:], F32)

    def step(cur):
        if gate is not None:
            _gate_step(gate_s, stream.n_blk, *gate)
        xn = xn_ref[...]
        for j in range(EXPERT_CHUNK // MXU_WIDTH):
            hd_s[cur, :, j * MXU_WIDTH:(j + 1) * MXU_WIDTH] = lax.dot_general(
                xn, u_ref[j * MXU_WIDTH:(j + 1) * MXU_WIDTH, :].astype(BF16), _NT, preferred_element_type=F32)
        e = e_ref[...]
        i1, i2 = _split_expert(e)
        acc = hs_ref[...]
        for g in range(groups):
            got = jnp.take_along_axis(hd_s[1 - cur, :, g * PEER_NKEYS:(g + 1) * PEER_NKEYS], i2, axis=1)
            acc = jnp.where(i1 == (c - 1) * groups + g, got, acc)
        hs_ref[...] = acc

    for parity in range(2):
        pl.when(lax.rem(c, 2) == parity)(functools.partial(step, parity))


def _peer_up(xn_bf16, e, u_tab, gate_args=None):
    n, d = xn_bf16.shape
    n_exp = u_tab.shape[0]
    tn = 1024 if n % 1024 == 0 else (256 if n % 256 == 0 else n)
    n_chunks = n_exp // EXPERT_CHUNK
    tok = lambda i, c, *_: (i, 0)
    return _host_call(
        _peer_up_kernel, "peer_up", (n // tn, n_chunks + 1),
        [pl.BlockSpec((tn, d), tok), pl.BlockSpec((tn, PEER_SLOTS), tok),
         pl.BlockSpec((EXPERT_CHUNK, d), lambda i, c, *_: (jnp.minimum(c, n_chunks - 1), 0))],
        [pl.BlockSpec((tn, PEER_SLOTS), tok)], [jax.ShapeDtypeStruct((n, PEER_SLOTS), F32)],
        [pltpu.VMEM((2, tn, EXPERT_CHUNK), F32)], [xn_bf16, e, u_tab], gate_args, drain_steps=1)


TOKEN_BATCH = 2 * SUBLANES
DOWN_EXPERT_CHUNK = 1024


def _peer_down_kernel(tn, stream, *refs):
    (e_ref, gate_ref, hs_ref, v_ref, h_ref, gf_ref), (y_ref,), (act_s, a_s), block_gate = _split_host_refs(
        stream, refs, 6, 1)
    c = pl.program_id(1)
    nk = PEER_NKEYS
    groups = DOWN_EXPERT_CHUNK // nk
    if block_gate is not None:
        gate_s = stream.start_step((pl.program_id(0), c), block_gate[2])

    @pl.when(c == 0)
    def _():
        y_ref[...] = jnp.zeros(y_ref.shape, F32)
        hv = hs_ref[...]
        gelu = 0.5 * hv * (1.0 + lax.erf(hv * math.sqrt(0.5)))
        act_s[...] = gate_ref[...] * gelu
        sub = lax.broadcasted_iota(I32, (nk, PEER_SLOTS), 0)

        def scatter_token(n):
            e = e_ref[pl.ds(n, 1), :]
            a = act_s[pl.ds(n, 1), :]
            i1, i2 = _split_expert(e)
            pt = jnp.where(sub == i1, a, 0.0).astype(BF16)
            qt = jnp.where(sub == i2, 1.0, 0.0).astype(BF16)
            return lax.dot_general(pt, qt, _NT, preferred_element_type=F32)

        def batch(t, carry):
            n0 = pl.multiple_of(t * TOKEN_BATCH, TOKEN_BATCH)
            halves = []
            for s0 in range(0, TOKEN_BATCH, SUBLANES):
                per_token = jnp.stack([scatter_token(n0 + s0 + k) for k in range(SUBLANES)], axis=0)
                halves.append(jnp.swapaxes(per_token, 0, 1))
            a_s[:, pl.ds(n0, TOKEN_BATCH), :] = jnp.concatenate(halves, axis=1).astype(BF16)
            return carry

        lax.fori_loop(0, tn // TOKEN_BATCH, batch, 0)

    if block_gate is not None:
        _gate_step(gate_s, stream.n_blk, *block_gate)

    lhs = jnp.concatenate([a_s[c * groups + g] for g in range(groups)], axis=1)
    y_ref[...] += jnp.dot(lhs, v_ref[...], preferred_element_type=F32)

    @pl.when(c == pl.num_programs(1) - 1)
    def _():
        y_ref[...] = _rmsnorm(h_ref[...] + y_ref[...], gf_ref[...])


def _peer_down(e, gate, hs, v_bf16, h, g_final, gate_args=None):
    n, d = h.shape
    n_exp = v_bf16.shape[0]
    tn = 512 if n % 512 == 0 else n
    tok = lambda i, c, *_: (i, 0)
    return _host_call(
        functools.partial(_peer_down_kernel, tn), "peer_down", (n // tn, n_exp // DOWN_EXPERT_CHUNK),
        [pl.BlockSpec((tn, PEER_SLOTS), tok), pl.BlockSpec((tn, PEER_SLOTS), tok),
         pl.BlockSpec((tn, PEER_SLOTS), tok),
         pl.BlockSpec((DOWN_EXPERT_CHUNK, d), lambda i, c, *_: (c, 0)),
         pl.BlockSpec((tn, d), tok, pipeline_mode=pl.Buffered(1)),
         pl.BlockSpec((1, d), lambda i, c, *_: (0, 0), pipeline_mode=pl.Buffered(1))],
        [pl.BlockSpec((tn, d), tok)], [jax.ShapeDtypeStruct((n, d), F32)],
        [pltpu.VMEM((tn, PEER_SLOTS), F32), pltpu.VMEM((PEER_NKEYS, tn, PEER_NKEYS), BF16)],
        [e, gate, hs, v_bf16, h, g_final.reshape(1, d)], gate_args)


def _peer_and_final_norm(h, xn_bf16, pq, sub_keys, u_tab, v_bf16, g_final, gate_args=None):
    n = h.shape[0]
    n_pad = -(-n // ROUTE_TOKENS) * ROUTE_TOKENS
    pq_pad = pq if n_pad == n else jnp.pad(pq, ((0, n_pad - n), (0, 0)))
    e, gate = _route(pq_pad, sub_keys)
    e, gate = e[:n], gate[:n]
    if gate_args is None:
        (hs,) = _peer_up(xn_bf16, e, u_tab)
        (y,) = _peer_down(e, gate, hs, v_bf16, h, g_final)
        return y, None
    q_sample, cache_k, page_table = gate_args
    half = q_sample.shape[0] // 2
    hs, _, top_a = _peer_up(xn_bf16, e, u_tab, (q_sample[:half], cache_k, page_table[:half]))
    y, _, top_b = _peer_down(e, gate, hs, v_bf16, h, g_final, (q_sample[half:], cache_k, page_table[half:]))
    return y, _GateStream.top_blocks([top_a, top_b])


def kernel(x_prompt, x_sample, cache_k, cache_v, state_conv, page_table, rel_bias_table, norm_mix_g, w_in,
           w_dw, b_dw, conv_ln_g, conv_ln_b, w_out, norm_ffn_g, peer_wq, peer_sub_keys, peer_u, peer_v,
           final_norm_g):
    batch, seq, d = x_prompt.shape
    bs, s_new, _ = x_sample.shape
    depth = w_in.shape[0]
    assert s_new == 1 and depth == 1 and seq % MOBA_BLOCK == 0 and seq // MOBA_BLOCK >= MOBA_TOPK
    hist = CONV_WIDTH - 1
    blk = MOBA_BLOCK

    dist = np.arange(blk)[None, :] - np.arange(blk)[:, None]
    bkt_prompt = jnp.concatenate([_rel_bucket(dist), _rel_bucket(dist + blk)], axis=0)
    bias_prompt = _bias_lookup(rel_bias_table, bkt_prompt).reshape(N_HEADS, 2, blk, blk)
    bkt_last = jnp.broadcast_to(_rel_bucket(blk - np.arange(blk))[None, :], (SUBLANES, blk))
    bias_last = _bias_lookup(rel_bias_table, bkt_last)[:, 0, :]

    hp = x_prompt.reshape(batch * seq, d)
    hs_ = x_sample.reshape(bs, d)
    l = 0
    w_in_b = w_in[l].astype(BF16)
    w_out_b = w_out[l].astype(BF16)
    wq_b = peer_wq[l].astype(BF16)
    u_b = peer_u[l]
    v_b = peer_v[l].astype(BF16)

    qp, kp, vp, up = _proj(hp, norm_mix_g[l], w_in_b)
    qs, ks, vs, us = _proj(hs_, norm_mix_g[l], w_in_b)
    q3 = qs.reshape(bs, N_HEADS, HEAD_DIM)

    attn_p = _prompt_attention(qp, kp, vp, bias_prompt, rel_bias_table, batch, seq)
    conv_p = _conv_prompt(up, w_dw[l], b_dw[l], conv_ln_g[l], conv_ln_b[l], batch, seq)
    h_p, xn_p, pq_p = _mix_out(attn_p, conv_p, hp, w_out_b, norm_ffn_g[l], wq_b)
    y_p, top = _peer_and_final_norm(h_p, xn_p, pq_p, peer_sub_keys[l], u_b, v_b, final_norm_g,
                                    (q3, cache_k[l], page_table))

    attn_s = _sample_attention(q3, ks.reshape(q3.shape), vs.reshape(q3.shape), cache_k[l], cache_v[l],
                               page_table, top, rel_bias_table, bias_last)
    conv_s, state_new = _conv_step(state_conv[l], us, w_dw[l], b_dw[l], conv_ln_g[l], conv_ln_b[l])
    h_s, xn_s, pq_s = _mix_out(attn_s, conv_s, hs_, w_out_b, norm_ffn_g[l], wq_b)
    y_s, _ = _peer_and_final_norm(h_s, xn_s, pq_s, peer_sub_keys[l], u_b, v_b, final_norm_g)

    kv_p = (1, batch, seq, N_HEADS, HEAD_DIM)
    kv_s = (1, bs, 1, N_HEADS, HEAD_DIM)
    conv_prompt_new = up.reshape(batch, seq, -1)[:, seq - hist:, :][None]
    return (y_p.reshape(batch, seq, d), y_s.reshape(bs, 1, d),
            kp.reshape(kv_p), vp.reshape(kv_p), conv_prompt_new,
            ks.reshape(kv_s), vs.reshape(kv_s), state_new[None])
```
